```python
import math
import jax, jax.numpy as jnp
from jax import lax
import numpy as np

D_MODEL = 2048
BATCH = 4
SEQ = 2048
DEPTH = 2

EPS = 1e-6
N_BRANCH = 3
POOL_WINDOWS = (2, 4, 8, 16)
POOL_WIDTH = D_MODEL // 4
POOL_GROUP = POOL_WIDTH // len(POOL_WINDOWS)
SB_HEAD_DIM = 128
SB_WIDTH = 3 * D_MODEL // 8
SB_HEADS = SB_WIDTH // SB_HEAD_DIM
SB_BLOCK = 128
GDN_HEAD_DIM = 128
GDN_WIDTH = 3 * D_MODEL // 8
GDN_HEADS = GDN_WIDTH // GDN_HEAD_DIM
GDN_CONV = 4
GDN_CHUNK = 64
D_FF = 4 * D_MODEL
IN_SIZES = (POOL_WIDTH, 3 * SB_WIDTH, 3 * GDN_WIDTH, GDN_WIDTH, GDN_HEADS, GDN_HEADS, N_BRANCH * D_MODEL)
N_IN = sum(IN_SIZES)

kernel_name = 'hybrid_pool_stickbreak_gdn_block'


def rms_norm(x, gain):
    x32 = x.astype(jnp.float32)
    y = x32 * lax.rsqrt(jnp.mean(x32 * x32, axis=-1, keepdims=True) + EPS)
    return (y * gain.astype(jnp.float32)).astype(x.dtype)


def l2_normalize(x):
    return x * lax.rsqrt(jnp.sum(x * x, axis=-1, keepdims=True) + EPS)


def pool_mixer(p, w_group, scale):
    b, s, _ = p.shape
    p32 = p.astype(jnp.float32)
    csum = jnp.cumsum(p32, axis=1)
    n_seen = jnp.arange(1, s + 1, dtype=jnp.float32)[None, :, None]
    groups = []
    for g, w in enumerate(POOL_WINDOWS):
        sl = slice(g * POOL_GROUP, (g + 1) * POOL_GROUP)
        cg = csum[..., sl]
        lagged = jnp.pad(cg[:, :s - w], ((0, 0), (w, 0), (0, 0)))
        groups.append((cg - lagged) / jnp.minimum(n_seen, w) - p32[..., sl])
    d = jnp.stack(groups, axis=2)
    y = jnp.einsum('bsgc,gcd->bsgd', d, w_group.astype(jnp.float32)).reshape(b, s, POOL_WIDTH)
    return (y * scale.astype(jnp.float32)).astype(p.dtype)


def stick_breaking_attention(q, k, v):
    b, s, h, dh = q.shape
    q32 = q.astype(jnp.float32) * (dh ** -0.5)
    k32 = k.astype(jnp.float32)
    v32 = v.astype(jnp.float32)
    outs = []
    for start in range(0, s, SB_BLOCK):
        end = start + SB_BLOCK
        z = jnp.einsum('bqhd,bkhd->bhqk', q32[:, start:end], k32[:, :end])
        mask = jnp.arange(end)[None, :] < jnp.arange(start, end)[:, None]
        log_stay = jnp.where(mask, jax.nn.log_sigmoid(-z), 0.0)
        log_later = lax.cumsum(log_stay, axis=3, reverse=True) - log_stay
        a = jnp.where(mask, jnp.exp(jax.nn.log_sigmoid(z) + log_later), 0.0)
        outs.append(jnp.einsum('bhqk,bkhd->bqhd', a, v32[:, :end]))
    return jnp.concatenate(outs, axis=1).astype(q.dtype)


def short_causal_conv(x, w):
    k, c = w.shape
    y = lax.conv_general_dilated(x, w[:, None, :], window_strides=(1,), padding=[(k - 1, 0)],
                                 dimension_numbers=('NWC', 'WIO', 'NWC'), feature_group_count=c)
    return jax.nn.silu(y)


def to_chunks(t):
    b, s, h = t.shape[:3]
    t = t.reshape((b, s // GDN_CHUNK, GDN_CHUNK, h) + t.shape[3:])
    return jnp.moveaxis(t, 3, 1)


def gated_delta_rule(q, k, v, log_alpha, beta):
    b, s, h, dk = q.shape
    dv = v.shape[-1]
    qc, kc, vc = to_chunks(q), to_chunks(k), to_chunks(v)
    g = jnp.cumsum(to_chunks(log_alpha), axis=-1)
    bc = to_chunks(beta)
    incl = jnp.tril(jnp.ones((GDN_CHUNK, GDN_CHUNK), dtype=bool))
    strict = jnp.tril(jnp.ones((GDN_CHUNK, GDN_CHUNK), dtype=bool), k=-1)
    diff = g[..., :, None] - g[..., None, :]
    gamma = jnp.where(incl, jnp.exp(jnp.where(incl, diff, 0.0)), 0.0)
    kk = jnp.einsum('bhncd,bhnmd->bhncm', kc, kc)
    lower = jnp.where(strict, bc[..., :, None] * kk * gamma, 0.0)
    unit_lower = lower + jnp.eye(GDN_CHUNK, dtype=lower.dtype)
    rhs = jnp.concatenate([vc * bc[..., None], kc * (bc * jnp.exp(g))[..., None]], axis=-1)
    sol = lax.linalg.triangular_solve(unit_lower, rhs, left_side=True, lower=True, unit_diagonal=True)
    u, w = sol[..., :dv], sol[..., dv:]
    qk = jnp.einsum('bhncd,bhnmd->bhncm', qc, kc) * gamma
    q_dec = qc * jnp.exp(g)[..., None]
    k_dec = kc * jnp.exp(g[..., -1:] - g)[..., None]
    chunk_decay = jnp.exp(g[..., -1])

    def step(state, inp):
        u_n, w_n, qk_n, qd_n, kd_n, dec_n = inp
        v_new = u_n - jnp.einsum('bhck,bhkv->bhcv', w_n, state)
        o_n = jnp.einsum('bhck,bhkv->bhcv', qd_n, state) + jnp.einsum('bhcm,bhmv->bhcv', qk_n, v_new)
        state = state * dec_n[..., None, None] + jnp.einsum('bhck,bhcv->bhkv', kd_n, v_new)
        return state, o_n

    xs = tuple(jnp.moveaxis(t, 2, 0) for t in (u, w, qk, q_dec, k_dec, chunk_decay))
    state0 = jnp.zeros((b, h, dk, dv), jnp.float32)
    _, o = lax.scan(step, state0, xs)
    return o.transpose(1, 0, 3, 2, 4).reshape(b, s, h, dv)


def gdn_mixer(qkv, z, a, bg, conv_w, a_log, dt_bias, norm_gain):
    b, s, _ = qkv.shape
    qkv = short_causal_conv(qkv, conv_w).astype(jnp.float32)
    q, k, v = jnp.split(qkv, 3, axis=-1)
    q = l2_normalize(q.reshape(b, s, GDN_HEADS, GDN_HEAD_DIM)) * (GDN_HEAD_DIM ** -0.5)
    k = l2_normalize(k.reshape(b, s, GDN_HEADS, GDN_HEAD_DIM))
    v = v.reshape(b, s, GDN_HEADS, GDN_HEAD_DIM)
    log_alpha = -jnp.exp(a_log.astype(jnp.float32)) * jax.nn.softplus(a.astype(jnp.float32) + dt_bias.astype(jnp.float32))
    beta = jax.nn.sigmoid(bg.astype(jnp.float32))
    o = gated_delta_rule(q, k, v, log_alpha, beta)
    zh = z.astype(jnp.float32).reshape(b, s, GDN_HEADS, GDN_HEAD_DIM)
    o = rms_norm(o, norm_gain) * jax.nn.silu(zh)
    return o.reshape(b, s, GDN_WIDTH).astype(z.dtype)


def hybrid_mixer(u, w_in, pool_w, pool_scale, gdn_conv, gdn_a_log, gdn_dt_bias, gdn_norm,
                 w_pool_up, w_sb_up, w_gdn_up, w_out):
    b, s, _ = u.shape
    proj = u @ w_in
    offsets = [int(o) for o in np.cumsum(IN_SIZES)[:-1]]
    p, sb_qkv, gdn_qkv, gdn_z, gdn_a, gdn_b, gates = jnp.split(proj, offsets, axis=-1)
    y_pool = pool_mixer(p, pool_w, pool_scale)
    sq, sk, sv = (t.reshape(b, s, SB_HEADS, SB_HEAD_DIM) for t in jnp.split(sb_qkv, 3, axis=-1))
    y_sb = stick_breaking_attention(sq, sk, sv).reshape(b, s, SB_WIDTH)
    y_gdn = gdn_mixer(gdn_qkv, gdn_z, gdn_a, gdn_b, gdn_conv, gdn_a_log, gdn_dt_bias, gdn_norm)
    g_pool, g_sb, g_gdn = jnp.split(jax.nn.sigmoid(gates), N_BRANCH, axis=-1)
    merged = g_pool * (y_pool @ w_pool_up) + g_sb * (y_sb @ w_sb_up) + g_gdn * (y_gdn @ w_gdn_up)
    return merged @ w_out


def squared_relu_mlp(u, w_ff1, w_ff2):
    h = jax.nn.relu(u @ w_ff1)
    return (h * h) @ w_ff2


def setup_inputs(seed: int = 0) -> dict:
    key = jax.random.key(seed)
    ks = jax.random.split(key, 20)
    f32 = jnp.float32

    def nrm(k, shape, fan_in):
        return jax.random.normal(k, shape, f32) * (fan_in ** -0.5)

    def gain(k, shape):
        return 1.0 + 0.02 * jax.random.normal(k, shape, f32)

    dt = jnp.exp(jax.random.uniform(ks[7], (DEPTH, GDN_HEADS), f32, math.log(1e-3), math.log(1e-1)))
    return {
        'x': jax.random.normal(ks[0], (BATCH, SEQ, D_MODEL), f32),
        'attn_norm': gain(ks[1], (DEPTH, D_MODEL)),
        'w_in': nrm(ks[2], (DEPTH, D_MODEL, N_IN), D_MODEL),
        'pool_w': nrm(ks[3], (DEPTH, len(POOL_WINDOWS), POOL_GROUP, POOL_GROUP), POOL_GROUP),
        'pool_scale': gain(ks[4], (DEPTH, POOL_WIDTH)),
        'gdn_conv': nrm(ks[5], (DEPTH, GDN_CONV, 3 * GDN_WIDTH), GDN_CONV),
        'gdn_a_log': jnp.log(jax.random.uniform(ks[6], (DEPTH, GDN_HEADS), f32, 1.0, 16.0)),
        'gdn_dt_bias': dt + jnp.log(-jnp.expm1(-dt)),
        'gdn_norm': gain(ks[8], (DEPTH, GDN_HEAD_DIM)),
        'w_pool_up': nrm(ks[9], (DEPTH, POOL_WIDTH, D_MODEL), POOL_WIDTH),
        'w_sb_up': nrm(ks[10], (DEPTH, SB_WIDTH, D_MODEL), SB_WIDTH),
        'w_gdn_up': nrm(ks[11], (DEPTH, GDN_WIDTH, D_MODEL), GDN_WIDTH),
        'w_out': nrm(ks[12], (DEPTH, D_MODEL, D_MODEL), D_MODEL),
        'mlp_norm': gain(ks[13], (DEPTH, D_MODEL)),
        'w_ff1': nrm(ks[14], (DEPTH, D_MODEL, D_FF), D_MODEL),
        'w_ff2': nrm(ks[15], (DEPTH, D_FF, D_MODEL), D_FF),
        'final_norm': gain(ks[16], (D_MODEL,)),
    }


def reference(x, attn_norm, w_in, pool_w, pool_scale, gdn_conv, gdn_a_log, gdn_dt_bias, gdn_norm,
              w_pool_up, w_sb_up, w_gdn_up, w_out, mlp_norm, w_ff1, w_ff2, final_norm):
    for l in range(DEPTH):
        u = rms_norm(x, attn_norm[l])
        x = x + hybrid_mixer(u, w_in[l], pool_w[l], pool_scale[l], gdn_conv[l], gdn_a_log[l],
                             gdn_dt_bias[l], gdn_norm[l], w_pool_up[l], w_sb_up[l], w_gdn_up[l], w_out[l])
        x = x + squared_relu_mlp(rms_norm(x, mlp_norm[l]), w_ff1[l], w_ff2[l])
    return rms_norm(x, final_norm)
```

```python
import functools

import jax
import jax.numpy as jnp
from jax import lax
from jax.experimental import pallas as pl
from jax.experimental.pallas import tpu as pltpu

EPS = 1e-6
LANE = 128
VMEM_LIMIT = 56 * 1024 * 1024

POOL_WINDOWS = (2, 4, 8, 16)
HEAD_DIM = 128
N_HEADS = 6
GDN_CONV = 4
GDN_CHUNK = 64

BF16 = jnp.bfloat16
F32 = jnp.float32


def _cparams(*sem):
    return pltpu.CompilerParams(dimension_semantics=sem, vmem_limit_bytes=VMEM_LIMIT)


def _rms_norm(x, gain):
    return x * lax.rsqrt(jnp.mean(x * x, axis=-1, keepdims=True) + EPS) * gain


def _softplus(x):
    return jnp.maximum(x, 0.0) + jnp.log1p(jnp.exp(-jnp.abs(x)))


def _split3(x):
    h1 = x.astype(BF16)
    r1 = x - h1.astype(F32)
    h2 = r1.astype(BF16)
    h3 = (r1 - h2.astype(F32)).astype(BF16)
    return h1, h2, h3


def _dot(a, b):
    return jnp.dot(a, b, preferred_element_type=F32)


def _dot_nt(a, b):
    return lax.dot_general(a, b, (((1,), (1,)), ((), ())), preferred_element_type=F32)


def _dot_tn(a, b):
    return lax.dot_general(a, b, (((0,), (0,)), ((), ())), preferred_element_type=F32)


def _norm_matmul_kernel(x_ref, g_ref, w_ref, o_ref, u_ref):
    @pl.when(pl.program_id(1) == 0)
    def _():
        u_ref[...] = _rms_norm(x_ref[...], g_ref[...]).astype(BF16)

    o_ref[...] = _dot(u_ref[...], w_ref[...])


def norm_matmul(x, gain, w, tm=1024, tn=1024):
    t, d = x.shape
    n = w.shape[1]
    return pl.pallas_call(
        _norm_matmul_kernel,
        grid=(t // tm, n // tn),
        in_specs=[pl.BlockSpec((tm, d), lambda i, j: (i, 0)),
                  pl.BlockSpec((1, d), lambda i, j: (0, 0)),
                  pl.BlockSpec((d, tn), lambda i, j: (0, j))],
        out_specs=pl.BlockSpec((tm, tn), lambda i, j: (i, j)),
        out_shape=jax.ShapeDtypeStruct((t, n), F32),
        scratch_shapes=[pltpu.VMEM((tm, d), BF16)],
        compiler_params=_cparams("parallel", "arbitrary"),
        name="norm_in_proj",
    )(x, gain.reshape(1, d), w)


def _pool_kernel(p_ref, w_ref, s_ref, o_ref):
    s, width = p_ref.shape
    group = width // len(POOL_WINDOWS)
    row = lax.broadcasted_iota(jnp.int32, (s, group), 0)
    n_seen = (row + 1).astype(F32)

    def shifted(x, k):
        return jnp.where(row >= k, pltpu.roll(x, k, 0), 0.0)

    for g, win in enumerate(POOL_WINDOWS):
        p = p_ref[:, g * group:(g + 1) * group]
        acc = p
        k = 1
        while k < win:
            acc = acc + shifted(acc, k)
            k *= 2
        d = acc / jnp.minimum(n_seen, float(win)) - p
        y = _dot(d.astype(BF16), w_ref[g].astype(BF16))
        o_ref[:, g * group:(g + 1) * group] = (y * s_ref[:, g * group:(g + 1) * group]).astype(o_ref.dtype)


def pool_mixer(proj, pool_w, pool_scale, batch, seq, width, col):
    t = batch * seq
    n_g, group, _ = pool_w.shape
    col_blk = col * LANE // width
    return pl.pallas_call(
        _pool_kernel,
        grid=(batch,),
        in_specs=[pl.BlockSpec((seq, width), lambda b: (b, col_blk)),
                  pl.BlockSpec((n_g, group, group), lambda b: (0, 0, 0)),
                  pl.BlockSpec((1, width), lambda b: (0, 0))],
        out_specs=pl.BlockSpec((seq, width), lambda b: (b, 0)),
        out_shape=jax.ShapeDtypeStruct((t, width), BF16),
        compiler_params=_cparams("parallel"),
        name="pool_mixer",
    )(proj, pool_w, pool_scale.reshape(1, width))


def _sb_kernel(q_ref, k_ref, v_ref, o_ref, *, blk, scale):
    qi = pl.program_id(2)
    q = q_ref[...].astype(BF16)
    r = lax.broadcasted_iota(jnp.int32, (blk, blk), 0)
    c = lax.broadcasted_iota(jnp.int32, (blk, blk), 1)
    later_mat = (r > c).astype(BF16)
    causal = c < r

    def block(j, carry, masked):
        acc, run = carry
        start = pl.multiple_of(j * blk, blk)
        kb = k_ref[pl.ds(start, blk), :].astype(BF16)
        vb = v_ref[pl.ds(start, blk), :].astype(BF16)
        z = _dot_nt(q, kb) * scale
        log_stay = -_softplus(z)
        if masked:
            log_stay = jnp.where(causal, log_stay, 0.0)
        hi = log_stay.astype(BF16)
        lo = (log_stay - hi.astype(F32)).astype(BF16)
        later = _dot(hi, later_mat) + _dot(lo, later_mat) + run
        a = jnp.exp(log_stay + z + later)
        if masked:
            a = jnp.where(causal, a, 0.0)
        acc = acc + _dot(a.astype(BF16), vb)
        run = run + jnp.sum(log_stay, axis=1, keepdims=True)
        return acc, run

    carry = (jnp.zeros((blk, HEAD_DIM), F32), jnp.zeros((blk, 1), F32))
    carry = block(qi, carry, True)
    acc, _ = lax.fori_loop(0, qi, lambda i, cr: block(qi - 1 - i, cr, False), carry)
    o_ref[...] = acc.astype(o_ref.dtype)


def sb_attention(proj, batch, seq, q_col, k_col, v_col, blk=256):
    t = batch * seq
    nq = seq // blk
    kernel = functools.partial(_sb_kernel, blk=blk, scale=HEAD_DIM ** -0.5)
    return pl.pallas_call(
        kernel,
        grid=(batch, N_HEADS, nq),
        in_specs=[pl.BlockSpec((blk, HEAD_DIM), lambda b, h, i: (b * nq + i, q_col + h)),
                  pl.BlockSpec((seq, HEAD_DIM), lambda b, h, i: (b, k_col + h)),
                  pl.BlockSpec((seq, HEAD_DIM), lambda b, h, i: (b, v_col + h))],
        out_specs=pl.BlockSpec((blk, HEAD_DIM), lambda b, h, i: (b * nq + i, h)),
        out_shape=jax.ShapeDtypeStruct((t, N_HEADS * HEAD_DIM), BF16),
        compiler_params=_cparams("parallel", "parallel", "arbitrary"),
        name="sb_attention",
    )(proj, proj, proj)


def _gdn_kernel(qkv_ref, z_ref, ab_ref, abt_ref, conv_ref, prm_c_ref, prm_r_ref, gain_ref,
                o_ref, xs_ref, q_s, k_s, v_s, state_ref, *, ts):
    C = GDN_CHUNK
    W = N_HEADS * HEAD_DIM
    n_chunks = ts // C
    pad = 8

    @pl.when(pl.program_id(1) == 0)
    def _():
        xs_ref[0:pad, :] = jnp.zeros((pad, 3 * W), F32)
        state_ref[...] = jnp.zeros_like(state_ref)

    xs_ref[pad:pad + ts, :] = qkv_ref[...]
    y = jnp.zeros((ts, 3 * W), F32)
    for i in range(GDN_CONV):
        off = pad - (GDN_CONV - 1) + i
        y = y + xs_ref[off:off + ts, :] * conv_ref[i:i + 1, :]
    xs_ref[0:pad, :] = xs_ref[ts:ts + pad, :]
    y = y * jax.nn.sigmoid(y)

    for h in range(N_HEADS):
        qh = y[:, h * HEAD_DIM:(h + 1) * HEAD_DIM]
        kh = y[:, W + h * HEAD_DIM:W + (h + 1) * HEAD_DIM]
        qn = qh * lax.rsqrt(jnp.sum(qh * qh, axis=-1, keepdims=True) + EPS) * (HEAD_DIM ** -0.5)
        kn = kh * lax.rsqrt(jnp.sum(kh * kh, axis=-1, keepdims=True) + EPS)
        q_s[:, h * HEAD_DIM:(h + 1) * HEAD_DIM] = qn
        k_s[:, h * HEAD_DIM:(h + 1) * HEAD_DIM] = kn
    v_s[...] = y[:, 2 * W:]

    r2 = lax.broadcasted_iota(jnp.int32, (2 * C, 2 * C), 0)
    c2 = lax.broadcasted_iota(jnp.int32, (2 * C, 2 * C), 1)
    same = (r2 < C) == (c2 < C)
    incl = same & (c2 <= r2)
    strict = same & (c2 < r2)
    eye = (r2 == c2).astype(F32)
    tril2 = incl.astype(BF16)
    triu2 = (same & (r2 <= c2)).astype(BF16)
    row_first = lax.broadcasted_iota(jnp.int32, (2 * C, 1), 0) < C
    lane_first = lax.broadcasted_iota(jnp.int32, (1, 2 * C), 1) < C

    neg_a_c = -jnp.exp(prm_c_ref[0:1, :])
    dtb_c = prm_c_ref[1:2, :]
    neg_a_r = -jnp.exp(prm_r_ref[0])
    dtb_r = prm_r_ref[1]
    gain = gain_ref[...]

    def chunk(ci, _):
        start = pl.multiple_of(ci * C, C)
        rows = pl.ds(start, C)
        ab = ab_ref[rows, :]
        ab2 = jnp.concatenate([ab, ab], axis=0)
        a1, a2, a3 = _split3(neg_a_c * _softplus(ab2 + dtb_c))
        g_c = _dot(tril2, a1) + _dot(tril2, a2) + _dot(tril2, a3)
        beta_c = jax.nn.sigmoid(ab2)
        abt = abt_ref[0, ci]
        b1, b2, b3 = _split3(neg_a_r * _softplus(abt + dtb_r))
        g_r = _dot(b1, triu2) + _dot(b2, triu2) + _dot(b3, triu2)

        for hp in range(N_HEADS // 2):
            hs = (2 * hp, 2 * hp + 1)
            h0, h1 = hs
            sl = [slice(h * HEAD_DIM, (h + 1) * HEAD_DIM) for h in hs]
            q2 = jnp.concatenate([q_s[rows, s] for s in sl], axis=0)
            k2 = jnp.concatenate([k_s[rows, s] for s in sl], axis=0)
            v2 = jnp.concatenate([v_s[rows, s] for s in sl], axis=0)
            gc2 = jnp.where(row_first, g_c[:, h0:h0 + 1], g_c[:, h1:h1 + 1])
            bc2 = jnp.where(row_first, beta_c[:, N_HEADS + h0:N_HEADS + h0 + 1],
                            beta_c[:, N_HEADS + h1:N_HEADS + h1 + 1])
            gr2 = jnp.where(lane_first, g_r[h0:h0 + 1, :], g_r[h1:h1 + 1, :])
            glast = [g_c[C - 1:C, h0:h0 + 1], g_c[2 * C - 1:2 * C, h1:h1 + 1]]
            gl2 = jnp.where(row_first, glast[0], glast[1])

            k2b = k2.astype(BF16)
            gamma = jnp.where(incl, jnp.exp(jnp.where(incl, gc2 - gr2, 0.0)), 0.0)
            kk = _dot_nt(k2b, k2b)
            low = jnp.where(strict, bc2 * kk * gamma, 0.0)
            inv = eye - low
            pw = low
            for _ in range(5):
                pwb = pw.astype(BF16)
                pw = _dot(pwb, pwb)
                inv = inv + _dot(inv.astype(BF16), pw.astype(BF16))
            eg = jnp.exp(gc2)
            rhs = jnp.concatenate([v2 * bc2, k2 * (bc2 * eg)], axis=1)
            sol = rhs + _dot((inv - eye).astype(BF16), rhs.astype(BF16))
            u2, w2 = sol[:, :HEAD_DIM], sol[:, HEAD_DIM:]
            qk = _dot_nt(q2.astype(BF16), k2b) * gamma
            qd2 = (q2 * eg).astype(BF16)
            kd2 = (k2 * jnp.exp(gl2 - gc2)).astype(BF16)
            w2b = w2.astype(BF16)

            vn = []
            o_inter = []
            for n, h in enumerate(hs):
                st = state_ref[h].astype(BF16)
                half = slice(n * C, (n + 1) * C)
                vn.append(u2[half] - _dot(w2b[half], st))
                o_inter.append(_dot(qd2[half], st))
            vn2 = jnp.concatenate(vn, axis=0).astype(BF16)
            o2 = jnp.concatenate(o_inter, axis=0) + _dot(qk.astype(BF16), vn2)
            for n, h in enumerate(hs):
                half = slice(n * C, (n + 1) * C)
                state_ref[h] = state_ref[h] * jnp.exp(glast[n]) + _dot_tn(kd2[half], vn2[half])
                o = o2[half]
                zh = z_ref[rows, sl[n]]
                o = _rms_norm(o, gain) * (zh * jax.nn.sigmoid(zh))
                o_ref[rows, sl[n]] = o.astype(o_ref.dtype)
        return 0

    lax.fori_loop(0, n_chunks, chunk, 0)


def gdn_mixer(proj, abt, conv_w, prm_c, prm_r, norm_gain, batch, seq, qkv_col, z_col, ab_col, ts=256):
    t = batch * seq
    W = N_HEADS * HEAD_DIM
    ns = seq // ts
    qkv_blk = qkv_col * LANE // (3 * W)
    z_blk = z_col * LANE // W
    kernel = functools.partial(_gdn_kernel, ts=ts)
    return pl.pallas_call(
        kernel,
        grid=(batch, ns),
        in_specs=[pl.BlockSpec((ts, 3 * W), lambda b, s: (b * ns + s, qkv_blk)),
                  pl.BlockSpec((ts, W), lambda b, s: (b * ns + s, z_blk)),
                  pl.BlockSpec((ts, LANE), lambda b, s: (b * ns + s, ab_col)),
                  pl.BlockSpec((1, ts // GDN_CHUNK, 16, 2 * GDN_CHUNK), lambda b, s: (b, s, 0, 0)),
                  pl.BlockSpec((GDN_CONV, 3 * W), lambda b, s: (0, 0)),
                  pl.BlockSpec((8, LANE), lambda b, s: (0, 0)),
                  pl.BlockSpec((2, 16, 2 * GDN_CHUNK), lambda b, s: (0, 0, 0)),
                  pl.BlockSpec((1, HEAD_DIM), lambda b, s: (0, 0))],
        out_specs=pl.BlockSpec((ts, W), lambda b, s: (b * ns + s, 0)),
        out_shape=jax.ShapeDtypeStruct((t, W), BF16),
        scratch_shapes=[pltpu.VMEM((ts + 8, 3 * W), F32),
                        pltpu.VMEM((ts, W), F32),
                        pltpu.VMEM((ts, W), F32),
                        pltpu.VMEM((ts, W), F32),
                        pltpu.VMEM((N_HEADS, HEAD_DIM, HEAD_DIM), F32)],
        compiler_params=_cparams("parallel", "arbitrary"),
        name="gdn_mixer",
    )(proj, proj, proj, abt, conv_w, prm_c, prm_r, norm_gain.reshape(1, HEAD_DIM))


def _merge_kernel(yp_ref, ys_ref, yg_ref, gp_ref, gs_ref, gg_ref, wp_ref, ws_ref, wg_ref, o_ref):
    m = jax.nn.sigmoid(gp_ref[...]) * _dot(yp_ref[...], wp_ref[...])
    m = m + jax.nn.sigmoid(gs_ref[...]) * _dot(ys_ref[...], ws_ref[...])
    m = m + jax.nn.sigmoid(gg_ref[...]) * _dot(yg_ref[...], wg_ref[...])
    o_ref[...] = m.astype(o_ref.dtype)


def merge_branches(y_pool, y_sb, y_gdn, proj, w_pool_up, w_sb_up, w_gdn_up, gate_col, tm=1024, tn=512):
    t = y_pool.shape[0]
    d = w_pool_up.shape[1]
    nb = d // tn
    g0 = gate_col * LANE // tn

    def y_spec(y):
        return pl.BlockSpec((tm, y.shape[1]), lambda i, j: (i, 0))

    def w_spec(w):
        return pl.BlockSpec((w.shape[0], tn), lambda i, j: (0, j))

    def gate_spec(branch):
        return pl.BlockSpec((tm, tn), lambda i, j: (i, g0 + branch * nb + j))

    return pl.pallas_call(
        _merge_kernel,
        grid=(t // tm, nb),
        in_specs=[y_spec(y_pool), y_spec(y_sb), y_spec(y_gdn),
                  gate_spec(0), gate_spec(1), gate_spec(2),
                  w_spec(w_pool_up), w_spec(w_sb_up), w_spec(w_gdn_up)],
        out_specs=pl.BlockSpec((tm, tn), lambda i, j: (i, j)),
        out_shape=jax.ShapeDtypeStruct((t, d), BF16),
        compiler_params=_cparams("parallel", "arbitrary"),
        name="merge_branches",
    )(y_pool, y_sb, y_gdn, proj, proj, proj, w_pool_up, w_sb_up, w_gdn_up)


def _matmul_res_kernel(a_ref, w_ref, r_ref, o_ref):
    o_ref[...] = r_ref[...] + _dot(a_ref[...], w_ref[...])


def matmul_residual(a, w, res, tm=1024, tn=1024):
    t, k = a.shape
    n = w.shape[1]
    return pl.pallas_call(
        _matmul_res_kernel,
        grid=(t // tm, n // tn),
        in_specs=[pl.BlockSpec((tm, k), lambda i, j: (i, 0)),
                  pl.BlockSpec((k, tn), lambda i, j: (0, j)),
                  pl.BlockSpec((tm, tn), lambda i, j: (i, j))],
        out_specs=pl.BlockSpec((tm, tn), lambda i, j: (i, j)),
        out_shape=jax.ShapeDtypeStruct((t, n), F32),
        compiler_params=_cparams("parallel", "arbitrary"),
        name="out_proj_residual",
    )(a, w, res)


def _mlp_kernel(x_ref, g_ref, w1_ref, w2_ref, fg_ref, o_ref, u_ref, acc_ref, *, final_norm):
    j = pl.program_id(1)

    @pl.when(j == 0)
    def _():
        u_ref[...] = _rms_norm(x_ref[...], g_ref[...]).astype(BF16)
        acc_ref[...] = jnp.zeros_like(acc_ref)

    h = jnp.maximum(_dot(u_ref[...], w1_ref[...]), 0.0)
    acc_ref[...] += _dot((h * h).astype(BF16), w2_ref[...])

    @pl.when(j == pl.num_programs(1) - 1)
    def _():
        out = x_ref[...] + acc_ref[...]
        if final_norm:
            out = _rms_norm(out, fg_ref[...])
        o_ref[...] = out


def mlp_block(x, gain, w1, w2, final_gain, final_norm, tm=512, tf=512):
    t, d = x.shape
    f = w1.shape[1]
    kernel = functools.partial(_mlp_kernel, final_norm=final_norm)
    return pl.pallas_call(
        kernel,
        grid=(t // tm, f // tf),
        in_specs=[pl.BlockSpec((tm, d), lambda i, j: (i, 0)),
                  pl.BlockSpec((1, d), lambda i, j: (0, 0)),
                  pl.BlockSpec((d, tf), lambda i, j: (0, j)),
                  pl.BlockSpec((tf, d), lambda i, j: (j, 0)),
                  pl.BlockSpec((1, d), lambda i, j: (0, 0))],
        out_specs=pl.BlockSpec((tm, d), lambda i, j: (i, 0)),
        out_shape=jax.ShapeDtypeStruct((t, d), F32),
        scratch_shapes=[pltpu.VMEM((tm, d), BF16), pltpu.VMEM((tm, d), F32)],
        compiler_params=_cparams("parallel", "arbitrary"),
        name="mlp_block",
    )(x, gain.reshape(1, d), w1, w2, final_gain.reshape(1, d))


def kernel(x, attn_norm, w_in, pool_w, pool_scale, gdn_conv, gdn_a_log, gdn_dt_bias, gdn_norm,
           w_pool_up, w_sb_up, w_gdn_up, w_out, mlp_norm, w_ff1, w_ff2, final_norm):
    batch, seq, d = x.shape
    depth = w_in.shape[0]
    pool_width = pool_w.shape[1] * pool_w.shape[2]
    W = N_HEADS * HEAD_DIM
    o_pool, o_sb, o_gdn = 0, pool_width, pool_width + 3 * W
    o_z = o_gdn + 3 * W
    o_ab = o_z + W
    n_ab = 2 * N_HEADS
    o_gate = o_ab + n_ab
    ab_pad = 2 * LANE
    gdn_col = 0
    z_col = gdn_col + 3 * N_HEADS
    pool_col = z_col + N_HEADS
    sb_col = pool_col + pool_width // LANE
    ab_col = sb_col + 3 * N_HEADS
    gate_col = ab_col + ab_pad // LANE

    xt = x.reshape(batch * seq, d)
    for l in range(depth):
        w = w_in[l]
        w_packed = jnp.concatenate(
            [w[:, o_gdn:o_z], w[:, o_z:o_ab], w[:, o_pool:o_sb], w[:, o_sb:o_gdn],
             jnp.pad(w[:, o_ab:o_gate], ((0, 0), (0, ab_pad - n_ab))), w[:, o_gate:]],
            axis=1).astype(BF16)
        proj = norm_matmul(xt, attn_norm[l], w_packed)

        y_pool = pool_mixer(proj, pool_w[l], pool_scale[l], batch, seq, pool_width, pool_col)
        y_sb = sb_attention(proj, batch, seq, sb_col, sb_col + N_HEADS, sb_col + 2 * N_HEADS)

        ab = proj[:, ab_col * LANE:ab_col * LANE + 16]
        abt = ab.reshape(batch, seq // GDN_CHUNK, GDN_CHUNK, 16).transpose(0, 1, 3, 2)
        abt = jnp.concatenate([abt, abt], axis=-1)
        prm = jnp.stack([gdn_a_log[l], gdn_dt_bias[l]])
        prm_c = jnp.pad(prm, ((0, 6), (0, LANE - N_HEADS)))
        prm_r = jnp.broadcast_to(jnp.pad(prm, ((0, 0), (0, 16 - N_HEADS)))[:, :, None],
                                 (2, 16, 2 * GDN_CHUNK))
        y_gdn = gdn_mixer(proj, abt, gdn_conv[l], prm_c, prm_r, gdn_norm[l], batch, seq,
                          gdn_col, z_col, ab_col)

        merged = merge_branches(y_pool, y_sb, y_gdn, proj, w_pool_up[l].astype(BF16),
                                w_sb_up[l].astype(BF16), w_gdn_up[l].astype(BF16), gate_col)
        xt = matmul_residual(merged, w_out[l].astype(BF16), xt)
        xt = mlp_block(xt, mlp_norm[l], w_ff1[l].astype(BF16), w_ff2[l].astype(BF16),
                       final_norm, final_norm=(l == depth - 1))
    return xt.reshape(batch, seq, d)
```

```python
import functools

import jax
import jax.numpy as jnp
from jax import lax
from jax.experimental import pallas as pl
from jax.experimental.pallas import tpu as pltpu

EPS = 1e-6
LANE = 128
VMEM_LIMIT = 56 * 1024 * 1024

POOL_WINDOWS = (2, 4, 8, 16)
HEAD_DIM = 128
N_HEADS = 6
GDN_CONV = 4
GDN_CHUNK = 64

BF16 = jnp.bfloat16
F32 = jnp.float32


def _cparams(*sem):
    return pltpu.CompilerParams(dimension_semantics=sem, vmem_limit_bytes=VMEM_LIMIT)


def _rms_norm(x, gain):
    return x * lax.rsqrt(jnp.mean(x * x, axis=-1, keepdims=True) + EPS) * gain


def _softplus(x):
    return jnp.maximum(x, 0.0) + jnp.log1p(jnp.exp(-jnp.abs(x)))


def _split3(x):
    h1 = x.astype(BF16)
    r1 = x - h1.astype(F32)
    h2 = r1.astype(BF16)
    h3 = (r1 - h2.astype(F32)).astype(BF16)
    return h1, h2, h3


def _block_index(lane_col, width):
    assert (lane_col * LANE) % width == 0, (lane_col, width)
    return lane_col * LANE // width


def _dot(a, b):
    return jnp.dot(a, b, preferred_element_type=F32)


def _dot_nt(a, b):
    return lax.dot_general(a, b, (((1,), (1,)), ((), ())), preferred_element_type=F32)


class InProjLayout:
    def __init__(self, pool_width, d_model):
        w = N_HEADS * HEAD_DIM
        self.n_ab = 2 * N_HEADS
        self.src_pool, self.src_sb, self.src_gdn = 0, pool_width, pool_width + 3 * w
        self.src_z = self.src_gdn + 3 * w
        self.src_ab = self.src_z + w
        self.src_gate = self.src_ab + self.n_ab
        self.n_gate = 3 * d_model
        self.n_src = self.src_gate + self.n_gate
        self.ab_pad = 2 * LANE
        self.dst_gdn = 0
        self.dst_z = self.dst_gdn + 3 * N_HEADS
        self.dst_sb = self.dst_z + N_HEADS
        self.dst_ab = self.dst_sb + 3 * N_HEADS
        self.dst_pool = self.dst_ab + self.ab_pad // LANE
        self.dst_gate = self.dst_pool + pool_width // LANE
        self.n_dst = self.dst_gate * LANE + self.n_gate
        self.shift = self.src_gate % LANE
        self.gate_base = self.src_gate - self.shift
        assert self.src_ab % LANE == 0 and self.gate_base == self.src_ab and self.shift == self.n_ab
        self.tail_start = (self.n_src // LANE) * LANE


def _repack_kernel(w_ref, tail_ref, o_ref, *, lay):
    def copy(dst_col, src, n):
        o_ref[:, dst_col * LANE:dst_col * LANE + n] = w_ref[:, src:src + n].astype(BF16)

    w = N_HEADS * HEAD_DIM
    copy(lay.dst_gdn, lay.src_gdn, 3 * w)
    copy(lay.dst_z, lay.src_z, w)
    copy(lay.dst_pool, lay.src_pool, lay.src_sb - lay.src_pool)
    copy(lay.dst_sb, lay.src_sb, 3 * w)
    rows = w_ref.shape[0]
    lane = lax.broadcasted_iota(jnp.int32, (rows, LANE), 1)
    ab0 = lay.dst_ab * LANE
    o_ref[:, ab0:ab0 + LANE] = jnp.where(lane < lay.n_ab, w_ref[:, lay.src_ab:lay.src_ab + LANE], 0.0).astype(BF16)
    o_ref[:, ab0 + LANE:ab0 + lay.ab_pad] = jnp.zeros((rows, lay.ab_pad - LANE), BF16)

    def src_block(col):
        if col + LANE <= lay.tail_start:
            return w_ref[:, col:col + LANE]
        assert col == lay.tail_start
        return tail_ref[...]

    for c in range(lay.n_gate // LANE):
        col = lay.gate_base + c * LANE
        a = pltpu.roll(src_block(col), LANE - lay.shift, 1)
        b = pltpu.roll(src_block(col + LANE), LANE - lay.shift, 1)
        g0 = lay.dst_gate * LANE + c * LANE
        o_ref[:, g0:g0 + LANE] = jnp.where(lane < LANE - lay.shift, a, b).astype(BF16)


def repack_in_proj(w_in, lay, tk=256):
    depth, d, n_src = w_in.shape
    assert n_src == lay.n_src
    tail = jnp.pad(w_in[:, :, lay.tail_start:], ((0, 0), (0, 0), (0, LANE - (n_src - lay.tail_start))))
    return pl.pallas_call(
        functools.partial(_repack_kernel, lay=lay),
        grid=(depth, d // tk),
        in_specs=[pl.BlockSpec((None, tk, n_src), lambda l, i: (l, i, 0)),
                  pl.BlockSpec((None, tk, LANE), lambda l, i: (l, i, 0))],
        out_specs=pl.BlockSpec((None, tk, lay.n_dst), lambda l, i: (l, i, 0)),
        out_shape=jax.ShapeDtypeStruct((depth, d, lay.n_dst), BF16),
        compiler_params=_cparams("parallel", "parallel"),
        name="repack_in_proj",
    )(w_in, tail)


def _norm_matmul_kernel(x_ref, g_ref, w_ref, o_ref, u_ref):
    @pl.when(pl.program_id(1) == 0)
    def _():
        u_ref[...] = _rms_norm(x_ref[...], g_ref[...]).astype(BF16)

    o_ref[...] = _dot(u_ref[...], w_ref[...])


def norm_matmul(x, gain, w, layer, tm=1024, tn=1024):
    t, d = x.shape
    n = w.shape[2]
    return pl.pallas_call(
        _norm_matmul_kernel,
        grid=(t // tm, n // tn),
        in_specs=[pl.BlockSpec((tm, d), lambda i, j: (i, 0)),
                  pl.BlockSpec((None, 1, d), lambda i, j: (layer, 0, 0)),
                  pl.BlockSpec((None, d, tn), lambda i, j: (layer, 0, j))],
        out_specs=pl.BlockSpec((tm, tn), lambda i, j: (i, j)),
        out_shape=jax.ShapeDtypeStruct((t, n), F32),
        scratch_shapes=[pltpu.VMEM((tm, d), BF16)],
        compiler_params=_cparams("parallel", "arbitrary"),
        name="norm_in_proj",
    )(x, gain.reshape(gain.shape[0], 1, d), w)


def _pool_kernel(p_ref, w_ref, s_ref, o_ref):
    s, width = p_ref.shape
    group = width // len(POOL_WINDOWS)
    row = lax.broadcasted_iota(jnp.int32, (s, group), 0)
    n_seen = (row + 1).astype(F32)

    def shifted(x, k):
        return jnp.where(row >= k, pltpu.roll(x, k, 0), 0.0)

    for g, win in enumerate(POOL_WINDOWS):
        p = p_ref[:, g * group:(g + 1) * group]
        acc = p
        k = 1
        while k < win:
            acc = acc + shifted(acc, k)
            k *= 2
        d = acc / jnp.minimum(n_seen, float(win)) - p
        y = _dot(d.astype(BF16), w_ref[g].astype(BF16))
        o_ref[:, g * group:(g + 1) * group] = (y * s_ref[:, g * group:(g + 1) * group]).astype(o_ref.dtype)


def pool_mixer(proj, pool_w, pool_scale, layer, batch, seq, width, col):
    t = batch * seq
    _, n_g, group, _ = pool_w.shape
    col_blk = _block_index(col, width)
    return pl.pallas_call(
        _pool_kernel,
        grid=(batch,),
        in_specs=[pl.BlockSpec((seq, width), lambda b: (b, col_blk)),
                  pl.BlockSpec((None, n_g, group, group), lambda b: (layer, 0, 0, 0)),
                  pl.BlockSpec((None, 1, width), lambda b: (layer, 0, 0))],
        out_specs=pl.BlockSpec((seq, width), lambda b: (b, 0)),
        out_shape=jax.ShapeDtypeStruct((t, width), BF16),
        compiler_params=_cparams("parallel"),
        name="pool_mixer",
    )(proj, pool_w, pool_scale.reshape(pool_scale.shape[0], 1, width))


def _sb_kernel(q_ref, k_ref, v_ref, o_ref, *, blk, scale, heads):
    qi = pl.program_id(2)
    r = lax.broadcasted_iota(jnp.int32, (blk, blk), 0)
    c = lax.broadcasted_iota(jnp.int32, (blk, blk), 1)
    later_mat = (r > c).astype(BF16)
    causal = c < r
    qs = [(q_ref[:, h * HEAD_DIM:(h + 1) * HEAD_DIM] * -scale).astype(BF16) for h in range(heads)]

    def block(j, carry, masked):
        start = pl.multiple_of(j * blk, blk)
        hd = [slice(h * HEAD_DIM, (h + 1) * HEAD_DIM) for h in range(heads)]
        nzs = [_dot_nt(qs[h], k_ref[pl.ds(start, blk), hd[h]].astype(BF16)) for h in range(heads)]
        stay, his, los = [], [], []
        for nz in nzs:
            log_stay = jnp.minimum(nz, 0.0) - jnp.log(1.0 + jnp.exp(-jnp.abs(nz)))
            if masked:
                log_stay = jnp.where(causal, log_stay, 0.0)
            hi = log_stay.astype(BF16)
            stay.append(log_stay)
            his.append(hi)
            los.append((log_stay - hi.astype(F32)).astype(BF16))
        laters = [_dot(his[h], later_mat) + _dot(los[h], later_mat) for h in range(heads)]
        accs, runs = [], []
        for h in range(heads):
            a = jnp.exp(stay[h] - nzs[h] + (laters[h] + carry[h][1]))
            if masked:
                a = jnp.where(causal, a, 0.0)
            accs.append(carry[h][0] + _dot(a.astype(BF16), v_ref[pl.ds(start, blk), hd[h]].astype(BF16)))
            runs.append(carry[h][1] + (laters[h][:, 0:1] + stay[h][:, 0:1]))
        return tuple(zip(accs, runs))

    zero = (jnp.zeros((blk, HEAD_DIM), F32), jnp.zeros((blk, 1), F32))
    carry = block(qi, (zero,) * heads, True)
    carry = lax.fori_loop(0, qi, lambda i, cr: block(qi - 1 - i, cr, False), carry)
    for h in range(heads):
        o_ref[:, h * HEAD_DIM:(h + 1) * HEAD_DIM] = carry[h][0].astype(o_ref.dtype)


def sb_attention(proj, batch, seq, q_col, k_col, v_col, blk=256, heads=N_HEADS):
    t = batch * seq
    nq = seq // blk
    width = heads * HEAD_DIM
    qb, kb, vb = (_block_index(col, width) for col in (q_col, k_col, v_col))
    kernel = functools.partial(_sb_kernel, blk=blk, scale=HEAD_DIM ** -0.5, heads=heads)
    return pl.pallas_call(
        kernel,
        grid=(batch, N_HEADS // heads, nq),
        in_specs=[pl.BlockSpec((blk, width), lambda b, h, i: (b * nq + i, qb + h)),
                  pl.BlockSpec((seq, width), lambda b, h, i: (b, kb + h)),
                  pl.BlockSpec((seq, width), lambda b, h, i: (b, vb + h))],
        out_specs=pl.BlockSpec((blk, width), lambda b, h, i: (b * nq + i, h)),
        out_shape=jax.ShapeDtypeStruct((t, N_HEADS * HEAD_DIM), BF16),
        compiler_params=_cparams("parallel", "parallel", "arbitrary"),
        name="sb_attention",
    )(proj, proj, proj)


def _gdn_kernel(qkv_ref, z_ref, ab_ref, conv_ref, prm_ref, gain_ref, o_ref,
                xs_ref, q_s, k_s, v_s, u_s, o_s, wq_s, qk_s, kdt_s, dec_s, state_ref, *, ts):
    C = GDN_CHUNK
    W = N_HEADS * HEAD_DIM
    n_chunks = ts // C
    n_pairs = N_HEADS // 2
    pad = 8

    @pl.when(pl.program_id(1) == 0)
    def _():
        xs_ref[0:pad, :] = jnp.zeros((pad, 3 * W), F32)
        state_ref[...] = jnp.zeros_like(state_ref)

    xs_ref[pad:pad + ts, :] = qkv_ref[...]
    y = jnp.zeros((ts, 3 * W), F32)
    for i in range(GDN_CONV):
        off = pad - (GDN_CONV - 1) + i
        y = y + xs_ref[off:off + ts, :] * conv_ref[i:i + 1, :]
    xs_ref[0:pad, :] = xs_ref[ts:ts + pad, :]
    y = y * jax.nn.sigmoid(y)
    for h in range(N_HEADS):
        hd = slice(h * HEAD_DIM, (h + 1) * HEAD_DIM)
        qh = y[:, hd]
        kh = y[:, W + h * HEAD_DIM:W + (h + 1) * HEAD_DIM]
        q_s[:, hd] = qh * lax.rsqrt(jnp.sum(qh * qh, axis=-1, keepdims=True) + EPS) * (HEAD_DIM ** -0.5)
        k_s[:, hd] = kh * lax.rsqrt(jnp.sum(kh * kh, axis=-1, keepdims=True) + EPS)
    v_s[...] = y[:, 2 * W:]

    ab = ab_ref[...]
    log_alpha = -jnp.exp(prm_ref[0:1, :]) * _softplus(ab + prm_ref[1:2, :])
    rt = lax.broadcasted_iota(jnp.int32, (ts, ts), 0)
    ct = lax.broadcasted_iota(jnp.int32, (ts, ts), 1)
    chunk_tril = ((rt // C == ct // C) & (ct <= rt)).astype(BF16)
    a1, a2, a3 = _split3(log_alpha)
    g_all = _dot(chunk_tril, a1) + _dot(chunk_tril, a2) + _dot(chunk_tril, a3)
    beta_all = jax.nn.sigmoid(ab)

    r2 = lax.broadcasted_iota(jnp.int32, (2 * C, 2 * C), 0)
    c2 = lax.broadcasted_iota(jnp.int32, (2 * C, 2 * C), 1)
    same = (r2 < C) == (c2 < C)
    incl = same & (c2 <= r2)
    strict = same & (c2 < r2)
    eye = (r2 == c2).astype(F32)
    row_first = r2 < C
    lane_first = c2 < C
    lane_first_row = lax.broadcasted_iota(jnp.int32, (1, 2 * C), 1) < C

    hd = [slice(h * HEAD_DIM, (h + 1) * HEAD_DIM) for h in range(N_HEADS)]
    inst = [(ci, p) for ci in range(n_chunks) for p in range(n_pairs)]
    g_ts = []
    for ci in range(n_chunks):
        g_c = g_all[ci * C:(ci + 1) * C]
        g_t = jnp.concatenate([g_c, g_c], axis=0).T
        dec_s[ci] = jnp.broadcast_to(jnp.exp(g_t[0:8, C - 1:C]), (8, LANE))
        g_ts.append(g_t)

    def pair_rows(ref, ci, p):
        rows = slice(ci * C, (ci + 1) * C)
        return jnp.concatenate([ref[rows, hd[2 * p]], ref[rows, hd[2 * p + 1]]], axis=0)

    def col_bcast(x, ci, p, lane0):
        x = x[ci * C:(ci + 1) * C]
        return jnp.concatenate(
            [jnp.broadcast_to(x[:, lane0 + h:lane0 + h + 1], (C, 2 * C)) for h in (2 * p, 2 * p + 1)], axis=0)

    gb = [col_bcast(g_all, ci, p, 0) for ci, p in inst]
    bb = [col_bcast(beta_all, ci, p, N_HEADS) for ci, p in inst]
    gr = [jnp.where(lane_first_row, g_ts[ci][2 * p:2 * p + 1, :], g_ts[ci][2 * p + 1:2 * p + 2, :])
          for ci, p in inst]
    gamma = [jnp.where(incl, jnp.exp(jnp.where(incl, gb[i] - gr[i], 0.0)), 0.0) for i in range(len(inst))]
    k2t = [pair_rows(k_s, ci, p).T for ci, p in inst]
    kq = [_dot(jnp.concatenate([pair_rows(k_s, ci, p) * bb[i], pair_rows(q_s, ci, p)], axis=0).astype(BF16),
               k2t[i].astype(BF16)) for i, (ci, p) in enumerate(inst)]
    pw = [jnp.where(strict, kq[i][:2 * C] * gamma[i], 0.0) for i in range(len(inst))]
    inv = [eye - x for x in pw]
    for _ in range(5):
        pw = [_dot(x.astype(BF16), x.astype(BF16)) for x in pw]
        inv = [y + _dot(y.astype(BF16), x.astype(BF16)) for y, x in zip(inv, pw)]
    eg = [jnp.exp(x) for x in gb]
    rhs = [jnp.concatenate([pair_rows(v_s, ci, p) * bb[i], pair_rows(k_s, ci, p) * (bb[i] * eg[i])], axis=1)
           for i, (ci, p) in enumerate(inst)]
    sol = [rhs[i] + _dot((inv[i] - eye).astype(BF16), rhs[i].astype(BF16)) for i in range(len(inst))]
    for i, (ci, p) in enumerate(inst):
        rows = slice(ci * C, (ci + 1) * C)
        w2 = sol[i][:, HEAD_DIM:].astype(BF16)
        qd2 = (pair_rows(q_s, ci, p) * eg[i]).astype(BF16)
        qk_s[ci, p] = (kq[i][2 * C:] * gamma[i]).astype(BF16)
        g_last = jnp.where(lane_first_row, g_ts[ci][2 * p:2 * p + 1, C - 1:C], g_ts[ci][2 * p + 1:2 * p + 2, C - 1:C])
        kdt = k2t[i] * jnp.exp(g_last - gr[i])
        for n, h in enumerate((2 * p, 2 * p + 1)):
            half = slice(n * C, (n + 1) * C)
            u_s[rows, hd[h]] = sol[i][half, :HEAD_DIM]
            wq_s[ci, h] = jnp.concatenate([w2[half], qd2[half]], axis=0)
            kdt_s[ci, h] = jnp.where(lane_first if n == 0 else ~lane_first, kdt, 0.0).astype(BF16)

    def chunk(ci, _):
        rows = pl.ds(pl.multiple_of(ci * C, C), C)
        dec = dec_s[ci]
        heads = range(N_HEADS)
        rs = [_dot(wq_s[ci, h], state_ref[h].astype(BF16)) for h in heads]
        vn2 = [jnp.concatenate([u_s[rows, hd[h]] - rs[h][:C] for h in (2 * p, 2 * p + 1)], axis=0).astype(BF16)
               for p in range(n_pairs)]
        o2 = [jnp.concatenate([rs[2 * p][C:], rs[2 * p + 1][C:]], axis=0) + _dot(qk_s[ci, p], vn2[p])
              for p in range(n_pairs)]
        new_state = [state_ref[h] * dec[h:h + 1, :] + _dot(kdt_s[ci, h], vn2[h // 2]) for h in heads]
        for h in heads:
            state_ref[h] = new_state[h]
            o_s[rows, hd[h]] = o2[h // 2][(h % 2) * C:(h % 2 + 1) * C]
        return 0

    lax.fori_loop(0, n_chunks, chunk, 0)

    gain = gain_ref[...]
    for h in range(N_HEADS):
        hd = slice(h * HEAD_DIM, (h + 1) * HEAD_DIM)
        zh = z_ref[:, hd]
        o_ref[:, hd] = (_rms_norm(o_s[:, hd], gain) * (zh * jax.nn.sigmoid(zh))).astype(o_ref.dtype)


def gdn_mixer(proj, conv_w, prm, norm_gain, layer, batch, seq, qkv_col, z_col, ab_col, ts=256):
    t = batch * seq
    W = N_HEADS * HEAD_DIM
    ns = seq // ts
    nc = ts // GDN_CHUNK
    qkv_blk = _block_index(qkv_col, 3 * W)
    z_blk = _block_index(z_col, W)
    kernel = functools.partial(_gdn_kernel, ts=ts)
    return pl.pallas_call(
        kernel,
        grid=(batch, ns),
        in_specs=[pl.BlockSpec((ts, 3 * W), lambda b, s: (b * ns + s, qkv_blk)),
                  pl.BlockSpec((ts, W), lambda b, s: (b * ns + s, z_blk)),
                  pl.BlockSpec((ts, LANE), lambda b, s: (b * ns + s, ab_col)),
                  pl.BlockSpec((None, GDN_CONV, 3 * W), lambda b, s: (layer, 0, 0)),
                  pl.BlockSpec((None, 8, LANE), lambda b, s: (layer, 0, 0)),
                  pl.BlockSpec((None, 1, HEAD_DIM), lambda b, s: (layer, 0, 0))],
        out_specs=pl.BlockSpec((ts, W), lambda b, s: (b * ns + s, 0)),
        out_shape=jax.ShapeDtypeStruct((t, W), BF16),
        scratch_shapes=[pltpu.VMEM((ts + 8, 3 * W), F32),
                        pltpu.VMEM((ts, W), F32), pltpu.VMEM((ts, W), F32),
                        pltpu.VMEM((ts, W), F32), pltpu.VMEM((ts, W), F32),
                        pltpu.VMEM((ts, W), F32),
                        pltpu.VMEM((nc, N_HEADS, 2 * GDN_CHUNK, HEAD_DIM), BF16),
                        pltpu.VMEM((nc, N_HEADS // 2, 2 * GDN_CHUNK, 2 * GDN_CHUNK), BF16),
                        pltpu.VMEM((nc, N_HEADS, HEAD_DIM, 2 * GDN_CHUNK), BF16),
                        pltpu.VMEM((nc, 8, LANE), F32),
                        pltpu.VMEM((N_HEADS, HEAD_DIM, HEAD_DIM), F32)],
        compiler_params=_cparams("parallel", "arbitrary"),
        name="gdn_mixer",
    )(proj, proj, proj, conv_w, prm, norm_gain.reshape(norm_gain.shape[0], 1, HEAD_DIM))


def _merge_kernel(yp_ref, ys_ref, yg_ref, gp_ref, gs_ref, gg_ref, wp_ref, ws_ref, wg_ref, o_ref):
    m = jax.nn.sigmoid(gp_ref[...]) * _dot(yp_ref[...], wp_ref[...])
    m = m + jax.nn.sigmoid(gs_ref[...]) * _dot(ys_ref[...], ws_ref[...])
    m = m + jax.nn.sigmoid(gg_ref[...]) * _dot(yg_ref[...], wg_ref[...])
    o_ref[...] = m.astype(o_ref.dtype)


def merge_branches(y_pool, y_sb, y_gdn, proj, w_pool_up, w_sb_up, w_gdn_up, layer, gate_col, tm=1024, tn=512):
    t = y_pool.shape[0]
    d = w_pool_up.shape[2]
    nb = d // tn
    g0 = _block_index(gate_col, tn)

    def y_spec(y):
        return pl.BlockSpec((tm, y.shape[1]), lambda i, j: (i, 0))

    def w_spec(w):
        return pl.BlockSpec((None, w.shape[1], tn), lambda i, j: (layer, 0, j))

    def gate_spec(branch):
        return pl.BlockSpec((tm, tn), lambda i, j: (i, g0 + branch * nb + j))

    return pl.pallas_call(
        _merge_kernel,
        grid=(t // tm, nb),
        in_specs=[y_spec(y_pool), y_spec(y_sb), y_spec(y_gdn),
                  gate_spec(0), gate_spec(1), gate_spec(2),
                  w_spec(w_pool_up), w_spec(w_sb_up), w_spec(w_gdn_up)],
        out_specs=pl.BlockSpec((tm, tn), lambda i, j: (i, j)),
        out_shape=jax.ShapeDtypeStruct((t, d), BF16),
        compiler_params=_cparams("parallel", "arbitrary"),
        name="merge_branches",
    )(y_pool, y_sb, y_gdn, proj, proj, proj, w_pool_up, w_sb_up, w_gdn_up)


def _matmul_res_kernel(a_ref, w_ref, r_ref, o_ref):
    o_ref[...] = r_ref[...] + _dot(a_ref[...], w_ref[...])


def matmul_residual(a, w, res, layer, tm=1024, tn=1024):
    t, k = a.shape
    n = w.shape[2]
    return pl.pallas_call(
        _matmul_res_kernel,
        grid=(t // tm, n // tn),
        in_specs=[pl.BlockSpec((tm, k), lambda i, j: (i, 0)),
                  pl.BlockSpec((None, k, tn), lambda i, j: (layer, 0, j)),
                  pl.BlockSpec((tm, tn), lambda i, j: (i, j))],
        out_specs=pl.BlockSpec((tm, tn), lambda i, j: (i, j)),
        out_shape=jax.ShapeDtypeStruct((t, n), F32),
        compiler_params=_cparams("parallel", "arbitrary"),
        name="out_proj_residual",
    )(a, w, res)


def _mlp_kernel(x_ref, g_ref, w1_ref, w2_ref, fg_ref, o_ref, u_ref, acc_ref, *, final_norm):
    j = pl.program_id(1)

    @pl.when(j == 0)
    def _():
        u_ref[...] = _rms_norm(x_ref[...], g_ref[...]).astype(BF16)
        acc_ref[...] = jnp.zeros_like(acc_ref)

    h = jnp.maximum(_dot(u_ref[...], w1_ref[...]), 0.0)
    acc_ref[...] += _dot((h * h).astype(BF16), w2_ref[...])

    @pl.when(j == pl.num_programs(1) - 1)
    def _():
        out = x_ref[...] + acc_ref[...]
        if final_norm:
            out = _rms_norm(out, fg_ref[...])
        o_ref[...] = out


def mlp_block(x, gain, w1, w2, final_gain, layer, final_norm, tm=512, tf=512):
    t, d = x.shape
    f = w1.shape[2]
    kernel = functools.partial(_mlp_kernel, final_norm=final_norm)
    return pl.pallas_call(
        kernel,
        grid=(t // tm, f // tf),
        in_specs=[pl.BlockSpec((tm, d), lambda i, j: (i, 0)),
                  pl.BlockSpec((None, 1, d), lambda i, j: (layer, 0, 0)),
                  pl.BlockSpec((None, d, tf), lambda i, j: (layer, 0, j)),
                  pl.BlockSpec((None, tf, d), lambda i, j: (layer, j, 0)),
                  pl.BlockSpec((1, d), lambda i, j: (0, 0))],
        out_specs=pl.BlockSpec((tm, d), lambda i, j: (i, 0)),
        out_shape=jax.ShapeDtypeStruct((t, d), F32),
        scratch_shapes=[pltpu.VMEM((tm, d), BF16), pltpu.VMEM((tm, d), F32)],
        compiler_params=_cparams("parallel", "arbitrary"),
        name="mlp_block",
    )(x, gain.reshape(gain.shape[0], 1, d), w1, w2, final_gain.reshape(1, d))


def kernel(x, attn_norm, w_in, pool_w, pool_scale, gdn_conv, gdn_a_log, gdn_dt_bias, gdn_norm,
           w_pool_up, w_sb_up, w_gdn_up, w_out, mlp_norm, w_ff1, w_ff2, final_norm):
    batch, seq, d = x.shape
    depth = w_in.shape[0]
    pool_width = pool_w.shape[1] * pool_w.shape[2]
    lay = InProjLayout(pool_width, d)

    w_in_p = repack_in_proj(w_in, lay)
    w_pool_up, w_sb_up, w_gdn_up, w_out, w_ff1, w_ff2 = (
        w.astype(BF16) for w in (w_pool_up, w_sb_up, w_gdn_up, w_out, w_ff1, w_ff2))
    gdn_prm = jnp.pad(jnp.stack([gdn_a_log, gdn_dt_bias], axis=1), ((0, 0), (0, 6), (0, LANE - N_HEADS)))

    xt = x.reshape(batch * seq, d)
    for l in range(depth):
        proj = norm_matmul(xt, attn_norm, w_in_p, l)
        y_pool = pool_mixer(proj, pool_w, pool_scale, l, batch, seq, pool_width, lay.dst_pool)
        y_sb = sb_attention(proj, batch, seq, lay.dst_sb, lay.dst_sb + N_HEADS, lay.dst_sb + 2 * N_HEADS)
        y_gdn = gdn_mixer(proj, gdn_conv, gdn_prm, gdn_norm, l, batch, seq, lay.dst_gdn, lay.dst_z, lay.dst_ab)
        merged = merge_branches(y_pool, y_sb, y_gdn, proj, w_pool_up, w_sb_up, w_gdn_up, l, lay.dst_gate)
        xt = matmul_residual(merged, w_out, xt, l)
        xt = mlp_block(xt, mlp_norm, w_ff1, w_ff2, final_norm, l, final_norm=(l == depth - 1))
    return xt.reshape(batch, seq, d)
```

```python
import functools

import jax
import jax.numpy as jnp
from jax import lax
from jax.experimental import pallas as pl
from jax.experimental.pallas import tpu as pltpu

EPS = 1e-6
LANE = 128
VMEM_LIMIT = 56 * 1024 * 1024

POOL_WINDOWS = (2, 4, 8, 16)
HEAD_DIM = 128
N_HEADS = 6
GDN_CONV = 4
GDN_CHUNK = 64

BF16 = jnp.bfloat16
F32 = jnp.float32


def _cparams(*sem):
    return pltpu.CompilerParams(dimension_semantics=sem, vmem_limit_bytes=VMEM_LIMIT)


def _rms_norm(x, gain):
    return x * lax.rsqrt(jnp.mean(x * x, axis=-1, keepdims=True) + EPS) * gain


def _softplus(x):
    return jnp.maximum(x, 0.0) + jnp.log1p(jnp.exp(-jnp.abs(x)))


def _split3(x):
    h1 = x.astype(BF16)
    r1 = x - h1.astype(F32)
    h2 = r1.astype(BF16)
    h3 = (r1 - h2.astype(F32)).astype(BF16)
    return h1, h2, h3


def _block_index(lane_col, width):
    assert (lane_col * LANE) % width == 0, (lane_col, width)
    return lane_col * LANE // width


def _dot(a, b):
    return jnp.dot(a, b, preferred_element_type=F32)


def _dot_nt(a, b):
    return lax.dot_general(a, b, (((1,), (1,)), ((), ())), preferred_element_type=F32)


class InProjLayout:
    def __init__(self, pool_width, d_model):
        w = N_HEADS * HEAD_DIM
        self.n_ab = 2 * N_HEADS
        self.src_pool, self.src_sb, self.src_gdn = 0, pool_width, pool_width + 3 * w
        self.src_z = self.src_gdn + 3 * w
        self.src_ab = self.src_z + w
        self.src_gate = self.src_ab + self.n_ab
        self.n_gate = 3 * d_model
        self.n_src = self.src_gate + self.n_gate
        self.ab_pad = 2 * LANE
        self.dst_gdn = 0
        self.dst_z = self.dst_gdn + 3 * N_HEADS
        self.dst_sb = self.dst_z + N_HEADS
        self.dst_ab = self.dst_sb + 3 * N_HEADS
        self.dst_pool = self.dst_ab + self.ab_pad // LANE
        self.dst_gate = self.dst_pool + pool_width // LANE
        self.n_dst = self.dst_gate * LANE + self.n_gate
        self.shift = self.src_gate % LANE
        self.gate_base = self.src_gate - self.shift
        assert self.src_ab % LANE == 0 and self.gate_base == self.src_ab and self.shift == self.n_ab
        self.tail_start = (self.n_src // LANE) * LANE


def _repack_kernel(w_ref, tail_ref, o_ref, *, lay):
    def copy(dst_col, src, n):
        o_ref[:, dst_col * LANE:dst_col * LANE + n] = w_ref[:, src:src + n]

    w = N_HEADS * HEAD_DIM
    copy(lay.dst_gdn, lay.src_gdn, 3 * w)
    copy(lay.dst_z, lay.src_z, w)
    copy(lay.dst_pool, lay.src_pool, lay.src_sb - lay.src_pool)
    copy(lay.dst_sb, lay.src_sb, 3 * w)
    rows = w_ref.shape[0]
    lane = lax.broadcasted_iota(jnp.int32, (rows, LANE), 1)
    ab0 = lay.dst_ab * LANE
    o_ref[:, ab0:ab0 + LANE] = jnp.where(lane < lay.n_ab, w_ref[:, lay.src_ab:lay.src_ab + LANE].astype(F32), 0.0).astype(BF16)
    o_ref[:, ab0 + LANE:ab0 + lay.ab_pad] = jnp.zeros((rows, lay.ab_pad - LANE), BF16)

    def src_block(col):
        if col + LANE <= lay.tail_start:
            return w_ref[:, col:col + LANE].astype(F32)
        assert col == lay.tail_start
        return tail_ref[...].astype(F32)

    for c in range(lay.n_gate // LANE):
        col = lay.gate_base + c * LANE
        a = pltpu.roll(src_block(col), LANE - lay.shift, 1)
        b = pltpu.roll(src_block(col + LANE), LANE - lay.shift, 1)
        g0 = lay.dst_gate * LANE + c * LANE
        o_ref[:, g0:g0 + LANE] = jnp.where(lane < LANE - lay.shift, a, b).astype(BF16)


def repack_in_proj(w_in, lay, tk=256):
    depth, d, n_src = w_in.shape
    assert n_src == lay.n_src
    w_in = w_in.astype(BF16)
    tail = jnp.pad(w_in[:, :, lay.tail_start:], ((0, 0), (0, 0), (0, LANE - (n_src - lay.tail_start))))
    return pl.pallas_call(
        functools.partial(_repack_kernel, lay=lay),
        grid=(depth, d // tk),
        in_specs=[pl.BlockSpec((None, tk, n_src), lambda l, i: (l, i, 0)),
                  pl.BlockSpec((None, tk, LANE), lambda l, i: (l, i, 0))],
        out_specs=pl.BlockSpec((None, tk, lay.n_dst), lambda l, i: (l, i, 0)),
        out_shape=jax.ShapeDtypeStruct((depth, d, lay.n_dst), BF16),
        compiler_params=_cparams("parallel", "parallel"),
        name="repack_in_proj",
    )(w_in, tail)


def _norm_matmul_kernel(x_ref, g_ref, w_ref, o_ref, u_ref):
    @pl.when(pl.program_id(1) == 0)
    def _():
        u_ref[...] = _rms_norm(x_ref[...], g_ref[...]).astype(BF16)

    o_ref[...] = _dot(u_ref[...], w_ref[...])


def norm_matmul(x, gain, w, layer, tm=1024, tn=1024):
    t, d = x.shape
    n = w.shape[2]
    return pl.pallas_call(
        _norm_matmul_kernel,
        grid=(t // tm, n // tn),
        in_specs=[pl.BlockSpec((tm, d), lambda i, j: (i, 0)),
                  pl.BlockSpec((None, 1, d), lambda i, j: (layer, 0, 0)),
                  pl.BlockSpec((None, d, tn), lambda i, j: (layer, 0, j))],
        out_specs=pl.BlockSpec((tm, tn), lambda i, j: (i, j)),
        out_shape=jax.ShapeDtypeStruct((t, n), F32),
        scratch_shapes=[pltpu.VMEM((tm, d), BF16)],
        compiler_params=_cparams("parallel", "arbitrary"),
        name="norm_in_proj",
    )(x, gain.reshape(gain.shape[0], 1, d), w)


def _pool_kernel(p_ref, w_ref, s_ref, o_ref):
    s, width = p_ref.shape
    group = width // len(POOL_WINDOWS)
    row = lax.broadcasted_iota(jnp.int32, (s, group), 0)
    n_seen = (row + 1).astype(F32)

    def shifted(x, k):
        return jnp.where(row >= k, pltpu.roll(x, k, 0), 0.0)

    for g, win in enumerate(POOL_WINDOWS):
        p = p_ref[:, g * group:(g + 1) * group]
        acc = p
        k = 1
        while k < win:
            acc = acc + shifted(acc, k)
            k *= 2
        d = acc / jnp.minimum(n_seen, float(win)) - p
        y = _dot(d.astype(BF16), w_ref[g].astype(BF16))
        o_ref[:, g * group:(g + 1) * group] = (y * s_ref[:, g * group:(g + 1) * group]).astype(o_ref.dtype)


def pool_mixer(proj, pool_w, pool_scale, layer, batch, seq, width, col):
    t = batch * seq
    _, n_g, group, _ = pool_w.shape
    col_blk = _block_index(col, width)
    return pl.pallas_call(
        _pool_kernel,
        grid=(batch,),
        in_specs=[pl.BlockSpec((seq, width), lambda b: (b, col_blk)),
                  pl.BlockSpec((None, n_g, group, group), lambda b: (layer, 0, 0, 0)),
                  pl.BlockSpec((None, 1, width), lambda b: (layer, 0, 0))],
        out_specs=pl.BlockSpec((seq, width), lambda b: (b, 0)),
        out_shape=jax.ShapeDtypeStruct((t, width), BF16),
        compiler_params=_cparams("parallel"),
        name="pool_mixer",
    )(proj, pool_w, pool_scale.reshape(pool_scale.shape[0], 1, width))


def _sb_kernel(q_ref, k_ref, v_ref, o_ref, *, blk, scale, heads):
    qi = pl.program_id(2)
    r = lax.broadcasted_iota(jnp.int32, (blk, blk), 0)
    c = lax.broadcasted_iota(jnp.int32, (blk, blk), 1)
    later_mat = (r > c).astype(BF16)
    causal = c < r
    qs = [(q_ref[:, h * HEAD_DIM:(h + 1) * HEAD_DIM] * -scale).astype(BF16) for h in range(heads)]

    def block(j, carry, masked):
        start = pl.multiple_of(j * blk, blk)
        hd = [slice(h * HEAD_DIM, (h + 1) * HEAD_DIM) for h in range(heads)]
        nzs = [_dot_nt(qs[h], k_ref[pl.ds(start, blk), hd[h]].astype(BF16)) for h in range(heads)]
        stay, his, los = [], [], []
        for nz in nzs:
            log_stay = jnp.minimum(nz, 0.0) - jnp.log(1.0 + jnp.exp(-jnp.abs(nz)))
            if masked:
                log_stay = jnp.where(causal, log_stay, 0.0)
            hi = log_stay.astype(BF16)
            stay.append(log_stay)
            his.append(hi)
            los.append((log_stay - hi.astype(F32)).astype(BF16))
        laters = [_dot(his[h], later_mat) + _dot(los[h], later_mat) for h in range(heads)]
        accs, runs = [], []
        for h in range(heads):
            a = jnp.exp(stay[h] - nzs[h] + (laters[h] + carry[h][1]))
            if masked:
                a = jnp.where(causal, a, 0.0)
            accs.append(carry[h][0] + _dot(a.astype(BF16), v_ref[pl.ds(start, blk), hd[h]].astype(BF16)))
            runs.append(carry[h][1] + (laters[h][:, 0:1] + stay[h][:, 0:1]))
        return tuple(zip(accs, runs))

    zero = (jnp.zeros((blk, HEAD_DIM), F32), jnp.zeros((blk, 1), F32))
    carry = block(qi, (zero,) * heads, True)
    carry = lax.fori_loop(0, qi, lambda i, cr: block(qi - 1 - i, cr, False), carry)
    for h in range(heads):
        o_ref[:, h * HEAD_DIM:(h + 1) * HEAD_DIM] = carry[h][0].astype(o_ref.dtype)


def sb_attention(proj, batch, seq, q_col, k_col, v_col, blk=256, heads=N_HEADS):
    t = batch * seq
    nq = seq // blk
    width = heads * HEAD_DIM
    qb, kb, vb = (_block_index(col, width) for col in (q_col, k_col, v_col))
    kernel = functools.partial(_sb_kernel, blk=blk, scale=HEAD_DIM ** -0.5, heads=heads)
    return pl.pallas_call(
        kernel,
        grid=(batch, N_HEADS // heads, nq),
        in_specs=[pl.BlockSpec((blk, width), lambda b, h, i: (b * nq + i, qb + h)),
                  pl.BlockSpec((seq, width), lambda b, h, i: (b, kb + h)),
                  pl.BlockSpec((seq, width), lambda b, h, i: (b, vb + h))],
        out_specs=pl.BlockSpec((blk, width), lambda b, h, i: (b * nq + i, h)),
        out_shape=jax.ShapeDtypeStruct((t, N_HEADS * HEAD_DIM), BF16),
        compiler_params=_cparams("parallel", "parallel", "arbitrary"),
        name="sb_attention",
    )(proj, proj, proj)


def _gdn_kernel(qkv_ref, z_ref, ab_ref, conv_ref, prm_ref, gain_ref, o_ref,
                xs_ref, q_s, k_s, v_s, u_s, o_s, wq_s, qk_s, kdt_s, dec_s, state_ref, *, ts):
    C = GDN_CHUNK
    W = N_HEADS * HEAD_DIM
    n_chunks = ts // C
    n_pairs = N_HEADS // 2
    pad = 8

    @pl.when(pl.program_id(1) == 0)
    def _():
        xs_ref[0:pad, :] = jnp.zeros((pad, 3 * W), F32)
        state_ref[...] = jnp.zeros_like(state_ref)

    xs_ref[pad:pad + ts, :] = qkv_ref[...]
    y = jnp.zeros((ts, 3 * W), F32)
    for i in range(GDN_CONV):
        off = pad - (GDN_CONV - 1) + i
        y = y + xs_ref[off:off + ts, :] * conv_ref[i:i + 1, :]
    xs_ref[0:pad, :] = xs_ref[ts:ts + pad, :]
    y = y * jax.nn.sigmoid(y)
    for h in range(N_HEADS):
        hd = slice(h * HEAD_DIM, (h + 1) * HEAD_DIM)
        qh = y[:, hd]
        kh = y[:, W + h * HEAD_DIM:W + (h + 1) * HEAD_DIM]
        q_s[:, hd] = qh * lax.rsqrt(jnp.sum(qh * qh, axis=-1, keepdims=True) + EPS) * (HEAD_DIM ** -0.5)
        k_s[:, hd] = kh * lax.rsqrt(jnp.sum(kh * kh, axis=-1, keepdims=True) + EPS)
    v_s[...] = y[:, 2 * W:]

    ab = ab_ref[...]
    log_alpha = -jnp.exp(prm_ref[0:1, :]) * _softplus(ab + prm_ref[1:2, :])
    rt = lax.broadcasted_iota(jnp.int32, (ts, ts), 0)
    ct = lax.broadcasted_iota(jnp.int32, (ts, ts), 1)
    chunk_tril = ((rt // C == ct // C) & (ct <= rt)).astype(BF16)
    a1, a2, a3 = _split3(log_alpha)
    g_all = _dot(chunk_tril, a1) + _dot(chunk_tril, a2) + _dot(chunk_tril, a3)
    beta_all = jax.nn.sigmoid(ab)

    r2 = lax.broadcasted_iota(jnp.int32, (2 * C, 2 * C), 0)
    c2 = lax.broadcasted_iota(jnp.int32, (2 * C, 2 * C), 1)
    same = (r2 < C) == (c2 < C)
    incl = same & (c2 <= r2)
    strict = same & (c2 < r2)
    eye = (r2 == c2).astype(F32)
    row_first = r2 < C
    lane_first = c2 < C
    lane_first_row = lax.broadcasted_iota(jnp.int32, (1, 2 * C), 1) < C

    hd = [slice(h * HEAD_DIM, (h + 1) * HEAD_DIM) for h in range(N_HEADS)]
    inst = [(ci, p) for ci in range(n_chunks) for p in range(n_pairs)]
    g_ts = []
    for ci in range(n_chunks):
        g_c = g_all[ci * C:(ci + 1) * C]
        g_t = jnp.concatenate([g_c, g_c], axis=0).T
        dec_s[ci] = jnp.broadcast_to(jnp.exp(g_t[0:8, C - 1:C]), (8, LANE))
        g_ts.append(g_t)

    def pair_rows(ref, ci, p):
        rows = slice(ci * C, (ci + 1) * C)
        return jnp.concatenate([ref[rows, hd[2 * p]], ref[rows, hd[2 * p + 1]]], axis=0)

    def col_bcast(x, ci, p, lane0):
        x = x[ci * C:(ci + 1) * C]
        return jnp.concatenate(
            [jnp.broadcast_to(x[:, lane0 + h:lane0 + h + 1], (C, 2 * C)) for h in (2 * p, 2 * p + 1)], axis=0)

    gb = [col_bcast(g_all, ci, p, 0) for ci, p in inst]
    bb = [col_bcast(beta_all, ci, p, N_HEADS) for ci, p in inst]
    gr = [jnp.where(lane_first_row, g_ts[ci][2 * p:2 * p + 1, :], g_ts[ci][2 * p + 1:2 * p + 2, :])
          for ci, p in inst]
    gamma = [jnp.where(incl, jnp.exp(jnp.where(incl, gb[i] - gr[i], 0.0)), 0.0) for i in range(len(inst))]
    k2t = [pair_rows(k_s, ci, p).T for ci, p in inst]
    kq = [_dot(jnp.concatenate([pair_rows(k_s, ci, p) * bb[i], pair_rows(q_s, ci, p)], axis=0).astype(BF16),
               k2t[i].astype(BF16)) for i, (ci, p) in enumerate(inst)]
    pw = [jnp.where(strict, kq[i][:2 * C] * gamma[i], 0.0) for i in range(len(inst))]
    inv = [eye - x for x in pw]
    for _ in range(5):
        pw = [_dot(x.astype(BF16), x.astype(BF16)) for x in pw]
        inv = [y + _dot(y.astype(BF16), x.astype(BF16)) for y, x in zip(inv, pw)]
    eg = [jnp.exp(x) for x in gb]
    rhs = [jnp.concatenate([pair_rows(v_s, ci, p) * bb[i], pair_rows(k_s, ci, p) * (bb[i] * eg[i])], axis=1)
           for i, (ci, p) in enumerate(inst)]
    sol = [rhs[i] + _dot((inv[i] - eye).astype(BF16), rhs[i].astype(BF16)) for i in range(len(inst))]
    for i, (ci, p) in enumerate(inst):
        rows = slice(ci * C, (ci + 1) * C)
        w2 = sol[i][:, HEAD_DIM:].astype(BF16)
        qd2 = (pair_rows(q_s, ci, p) * eg[i]).astype(BF16)
        qk_s[ci, p] = (kq[i][2 * C:] * gamma[i]).astype(BF16)
        g_last = jnp.where(lane_first_row, g_ts[ci][2 * p:2 * p + 1, C - 1:C], g_ts[ci][2 * p + 1:2 * p + 2, C - 1:C])
        kdt = k2t[i] * jnp.exp(g_last - gr[i])
        for n, h in enumerate((2 * p, 2 * p + 1)):
            half = slice(n * C, (n + 1) * C)
            u_s[rows, hd[h]] = sol[i][half, :HEAD_DIM]
            wq_s[ci, h] = jnp.concatenate([w2[half], qd2[half]], axis=0)
            kdt_s[ci, h] = jnp.where(lane_first if n == 0 else ~lane_first, kdt, 0.0).astype(BF16)

    def chunk(ci, _):
        rows = pl.ds(pl.multiple_of(ci * C, C), C)
        dec = dec_s[ci]
        heads = range(N_HEADS)
        rs = [_dot(wq_s[ci, h], state_ref[h].astype(BF16)) for h in heads]
        vn2 = [jnp.concatenate([u_s[rows, hd[h]] - rs[h][:C] for h in (2 * p, 2 * p + 1)], axis=0).astype(BF16)
               for p in range(n_pairs)]
        o2 = [jnp.concatenate([rs[2 * p][C:], rs[2 * p + 1][C:]], axis=0) + _dot(qk_s[ci, p], vn2[p])
              for p in range(n_pairs)]
        new_state = [state_ref[h] * dec[h:h + 1, :] + _dot(kdt_s[ci, h], vn2[h // 2]) for h in heads]
        for h in heads:
            state_ref[h] = new_state[h]
            o_s[rows, hd[h]] = o2[h // 2][(h % 2) * C:(h % 2 + 1) * C]
        return 0

    lax.fori_loop(0, n_chunks, chunk, 0)

    gain = gain_ref[...]
    for h in range(N_HEADS):
        hd = slice(h * HEAD_DIM, (h + 1) * HEAD_DIM)
        zh = z_ref[:, hd]
        o_ref[:, hd] = (_rms_norm(o_s[:, hd], gain) * (zh * jax.nn.sigmoid(zh))).astype(o_ref.dtype)


def gdn_mixer(proj, conv_w, prm, norm_gain, layer, batch, seq, qkv_col, z_col, ab_col, ts=256):
    t = batch * seq
    W = N_HEADS * HEAD_DIM
    ns = seq // ts
    nc = ts // GDN_CHUNK
    qkv_blk = _block_index(qkv_col, 3 * W)
    z_blk = _block_index(z_col, W)
    kernel = functools.partial(_gdn_kernel, ts=ts)
    return pl.pallas_call(
        kernel,
        grid=(batch, ns),
        in_specs=[pl.BlockSpec((ts, 3 * W), lambda b, s: (b * ns + s, qkv_blk)),
                  pl.BlockSpec((ts, W), lambda b, s: (b * ns + s, z_blk)),
                  pl.BlockSpec((ts, LANE), lambda b, s: (b * ns + s, ab_col)),
                  pl.BlockSpec((None, GDN_CONV, 3 * W), lambda b, s: (layer, 0, 0)),
                  pl.BlockSpec((None, 8, LANE), lambda b, s: (layer, 0, 0)),
                  pl.BlockSpec((None, 1, HEAD_DIM), lambda b, s: (layer, 0, 0))],
        out_specs=pl.BlockSpec((ts, W), lambda b, s: (b * ns + s, 0)),
        out_shape=jax.ShapeDtypeStruct((t, W), BF16),
        scratch_shapes=[pltpu.VMEM((ts + 8, 3 * W), F32),
                        pltpu.VMEM((ts, W), F32), pltpu.VMEM((ts, W), F32),
                        pltpu.VMEM((ts, W), F32), pltpu.VMEM((ts, W), F32),
                        pltpu.VMEM((ts, W), F32),
                        pltpu.VMEM((nc, N_HEADS, 2 * GDN_CHUNK, HEAD_DIM), BF16),
                        pltpu.VMEM((nc, N_HEADS // 2, 2 * GDN_CHUNK, 2 * GDN_CHUNK), BF16),
                        pltpu.VMEM((nc, N_HEADS, HEAD_DIM, 2 * GDN_CHUNK), BF16),
                        pltpu.VMEM((nc, 8, LANE), F32),
                        pltpu.VMEM((N_HEADS, HEAD_DIM, HEAD_DIM), F32)],
        compiler_params=_cparams("parallel", "arbitrary"),
        name="gdn_mixer",
    )(proj, proj, proj, conv_w, prm, norm_gain.reshape(norm_gain.shape[0], 1, HEAD_DIM))


def _merge_kernel(yp_ref, ys_ref, yg_ref, gp_ref, gs_ref, gg_ref, wp_ref, ws_ref, wg_ref, o_ref):
    m = jax.nn.sigmoid(gp_ref[...]) * _dot(yp_ref[...], wp_ref[...])
    m = m + jax.nn.sigmoid(gs_ref[...]) * _dot(ys_ref[...], ws_ref[...])
    m = m + jax.nn.sigmoid(gg_ref[...]) * _dot(yg_ref[...], wg_ref[...])
    o_ref[...] = m.astype(o_ref.dtype)


def merge_branches(y_pool, y_sb, y_gdn, proj, w_pool_up, w_sb_up, w_gdn_up, layer, gate_col, tm=1024, tn=512):
    t = y_pool.shape[0]
    d = w_pool_up.shape[2]
    nb = d // tn
    g0 = _block_index(gate_col, tn)

    def y_spec(y):
        return pl.BlockSpec((tm, y.shape[1]), lambda i, j: (i, 0))

    def w_spec(w):
        return pl.BlockSpec((None, w.shape[1], tn), lambda i, j: (layer, 0, j))

    def gate_spec(branch):
        return pl.BlockSpec((tm, tn), lambda i, j: (i, g0 + branch * nb + j))

    return pl.pallas_call(
        _merge_kernel,
        grid=(t // tm, nb),
        in_specs=[y_spec(y_pool), y_spec(y_sb), y_spec(y_gdn),
                  gate_spec(0), gate_spec(1), gate_spec(2),
                  w_spec(w_pool_up), w_spec(w_sb_up), w_spec(w_gdn_up)],
        out_specs=pl.BlockSpec((tm, tn), lambda i, j: (i, j)),
        out_shape=jax.ShapeDtypeStruct((t, d), BF16),
        compiler_params=_cparams("parallel", "arbitrary"),
        name="merge_branches",
    )(y_pool, y_sb, y_gdn, proj, proj, proj, w_pool_up, w_sb_up, w_gdn_up)


def _matmul_res_kernel(a_ref, w_ref, r_ref, o_ref):
    o_ref[...] = r_ref[...] + _dot(a_ref[...], w_ref[...])


def matmul_residual(a, w, res, layer, tm=1024, tn=1024):
    t, k = a.shape
    n = w.shape[2]
    return pl.pallas_call(
        _matmul_res_kernel,
        grid=(t // tm, n // tn),
        in_specs=[pl.BlockSpec((tm, k), lambda i, j: (i, 0)),
                  pl.BlockSpec((None, k, tn), lambda i, j: (layer, 0, j)),
                  pl.BlockSpec((tm, tn), lambda i, j: (i, j))],
        out_specs=pl.BlockSpec((tm, tn), lambda i, j: (i, j)),
        out_shape=jax.ShapeDtypeStruct((t, n), F32),
        compiler_params=_cparams("parallel", "arbitrary"),
        name="out_proj_residual",
    )(a, w, res)


def _mlp_kernel(x_ref, g_ref, w1_ref, w2_ref, fg_ref, o_ref, u_ref, acc_ref, *, final_norm):
    j = pl.program_id(1)

    @pl.when(j == 0)
    def _():
        u_ref[...] = _rms_norm(x_ref[...], g_ref[...]).astype(BF16)
        acc_ref[...] = jnp.zeros_like(acc_ref)

    h = jnp.maximum(_dot(u_ref[...], w1_ref[...]), 0.0)
    acc_ref[...] += _dot((h * h).astype(BF16), w2_ref[...])

    @pl.when(j == pl.num_programs(1) - 1)
    def _():
        out = x_ref[...] + acc_ref[...]
        if final_norm:
            out = _rms_norm(out, fg_ref[...])
        o_ref[...] = out


def mlp_block(x, gain, w1, w2, final_gain, layer, final_norm, tm=512, tf=1024):
    t, d = x.shape
    f = w1.shape[2]
    kernel = functools.partial(_mlp_kernel, final_norm=final_norm)
    return pl.pallas_call(
        kernel,
        grid=(t // tm, f // tf),
        in_specs=[pl.BlockSpec((tm, d), lambda i, j: (i, 0)),
                  pl.BlockSpec((None, 1, d), lambda i, j: (layer, 0, 0)),
                  pl.BlockSpec((None, d, tf), lambda i, j: (layer, 0, j)),
                  pl.BlockSpec((None, tf, d), lambda i, j: (layer, j, 0)),
                  pl.BlockSpec((1, d), lambda i, j: (0, 0))],
        out_specs=pl.BlockSpec((tm, d), lambda i, j: (i, 0)),
        out_shape=jax.ShapeDtypeStruct((t, d), F32),
        scratch_shapes=[pltpu.VMEM((tm, d), BF16), pltpu.VMEM((tm, d), F32)],
        compiler_params=_cparams("parallel", "arbitrary"),
        name="mlp_block",
    )(x, gain.reshape(gain.shape[0], 1, d), w1, w2, final_gain.reshape(1, d))


def kernel(x, attn_norm, w_in, pool_w, pool_scale, gdn_conv, gdn_a_log, gdn_dt_bias, gdn_norm,
           w_pool_up, w_sb_up, w_gdn_up, w_out, mlp_norm, w_ff1, w_ff2, final_norm):
    batch, seq, d = x.shape
    depth = w_in.shape[0]
    pool_width = pool_w.shape[1] * pool_w.shape[2]
    lay = InProjLayout(pool_width, d)

    w_in_p = repack_in_proj(w_in, lay)
    w_pool_up, w_sb_up, w_gdn_up, w_out, w_ff1, w_ff2 = (
        w.astype(BF16) for w in (w_pool_up, w_sb_up, w_gdn_up, w_out, w_ff1, w_ff2))
    gdn_prm = jnp.pad(jnp.stack([gdn_a_log, gdn_dt_bias], axis=1), ((0, 0), (0, 6), (0, LANE - N_HEADS)))

    xt = x.reshape(batch * seq, d)
    for l in range(depth):
        proj = norm_matmul(xt, attn_norm, w_in_p, l)
        y_pool = pool_mixer(proj, pool_w, pool_scale, l, batch, seq, pool_width, lay.dst_pool)
        y_sb = sb_attention(proj, batch, seq, lay.dst_sb, lay.dst_sb + N_HEADS, lay.dst_sb + 2 * N_HEADS)
        y_gdn = gdn_mixer(proj, gdn_conv, gdn_prm, gdn_norm, l, batch, seq, lay.dst_gdn, lay.dst_z, lay.dst_ab)
        merged = merge_branches(y_pool, y_sb, y_gdn, proj, w_pool_up, w_sb_up, w_gdn_up, l, lay.dst_gate)
        xt = matmul_residual(merged, w_out, xt, l)
        xt = mlp_block(xt, mlp_norm, w_ff1, w_ff2, final_norm, l, final_norm=(l == depth - 1))
    return xt.reshape(batch, seq, d)
```

```python
import functools

import jax
import jax.numpy as jnp
from jax import lax
from jax.experimental import pallas as pl
from jax.experimental.pallas import tpu as pltpu

EPS = 1e-6
LANE = 128
VMEM_LIMIT = 56 * 1024 * 1024

POOL_WINDOWS = (2, 4, 8, 16)
HEAD_DIM = 128
N_HEADS = 6
GDN_CONV = 4
GDN_CHUNK = 64

BF16 = jnp.bfloat16
F32 = jnp.float32


def _cparams(*sem):
    return pltpu.CompilerParams(dimension_semantics=sem, vmem_limit_bytes=VMEM_LIMIT)


def _rms_norm(x, gain):
    return x * lax.rsqrt(jnp.mean(x * x, axis=-1, keepdims=True) + EPS) * gain


def _softplus(x):
    return jnp.maximum(x, 0.0) + jnp.log1p(jnp.exp(-jnp.abs(x)))


def _split3(x):
    h1 = x.astype(BF16)
    r1 = x - h1.astype(F32)
    h2 = r1.astype(BF16)
    h3 = (r1 - h2.astype(F32)).astype(BF16)
    return h1, h2, h3


def _block_index(lane_col, width):
    assert (lane_col * LANE) % width == 0, (lane_col, width)
    return lane_col * LANE // width


def _dot(a, b):
    return jnp.dot(a, b, preferred_element_type=F32)


def _dot_nt(a, b):
    return lax.dot_general(a, b, (((1,), (1,)), ((), ())), preferred_element_type=F32)


CAST_BLOCK_BYTES = 8 * 1024 * 1024


def _cast_kernel(w_ref, o_ref):
    o_ref[...] = w_ref[...].astype(o_ref.dtype)


def cast_bf16(w):
    depth, rows, cols = w.shape
    tr = min(rows, CAST_BLOCK_BYTES // (cols * 4))
    assert rows % tr == 0
    return pl.pallas_call(
        _cast_kernel,
        grid=(depth, rows // tr),
        in_specs=[pl.BlockSpec((None, tr, cols), lambda l, i: (l, i, 0))],
        out_specs=pl.BlockSpec((None, tr, cols), lambda l, i: (l, i, 0)),
        out_shape=jax.ShapeDtypeStruct(w.shape, BF16),
        compiler_params=_cparams("parallel", "parallel"),
        name="cast_bf16",
    )(w)


class InProjLayout:
    def __init__(self, pool_width, d_model):
        w = N_HEADS * HEAD_DIM
        self.n_ab = 2 * N_HEADS
        self.src_pool, self.src_sb, self.src_gdn = 0, pool_width, pool_width + 3 * w
        self.src_z = self.src_gdn + 3 * w
        self.src_ab = self.src_z + w
        self.src_gate = self.src_ab + self.n_ab
        self.n_gate = 3 * d_model
        self.n_src = self.src_gate + self.n_gate
        self.ab_pad = 2 * LANE
        self.dst_gdn = 0
        self.dst_z = self.dst_gdn + 3 * N_HEADS
        self.dst_sb = self.dst_z + N_HEADS
        self.dst_ab = self.dst_sb + 3 * N_HEADS
        self.dst_pool = self.dst_ab + self.ab_pad // LANE
        self.dst_gate = self.dst_pool + pool_width // LANE
        self.n_dst = self.dst_gate * LANE + self.n_gate
        self.shift = self.src_gate % LANE
        self.gate_base = self.src_gate - self.shift
        assert self.src_ab % LANE == 0 and self.gate_base == self.src_ab and self.shift == self.n_ab
        self.tail_start = (self.n_src // LANE) * LANE


def _repack_kernel(w_ref, tail_ref, o_ref, *, lay):
    def copy(dst_col, src, n):
        o_ref[:, dst_col * LANE:dst_col * LANE + n] = w_ref[:, src:src + n]

    w = N_HEADS * HEAD_DIM
    copy(lay.dst_gdn, lay.src_gdn, 3 * w)
    copy(lay.dst_z, lay.src_z, w)
    copy(lay.dst_pool, lay.src_pool, lay.src_sb - lay.src_pool)
    copy(lay.dst_sb, lay.src_sb, 3 * w)
    rows = w_ref.shape[0]
    lane = lax.broadcasted_iota(jnp.int32, (rows, LANE), 1)
    ab0 = lay.dst_ab * LANE
    o_ref[:, ab0:ab0 + LANE] = jnp.where(lane < lay.n_ab, w_ref[:, lay.src_ab:lay.src_ab + LANE].astype(F32), 0.0).astype(BF16)
    o_ref[:, ab0 + LANE:ab0 + lay.ab_pad] = jnp.zeros((rows, lay.ab_pad - LANE), BF16)

    def src_block(col):
        if col + LANE <= lay.tail_start:
            return w_ref[:, col:col + LANE].astype(F32)
        assert col == lay.tail_start
        return tail_ref[...].astype(F32)

    for c in range(lay.n_gate // LANE):
        col = lay.gate_base + c * LANE
        a = pltpu.roll(src_block(col), LANE - lay.shift, 1)
        b = pltpu.roll(src_block(col + LANE), LANE - lay.shift, 1)
        g0 = lay.dst_gate * LANE + c * LANE
        o_ref[:, g0:g0 + LANE] = jnp.where(lane < LANE - lay.shift, a, b).astype(BF16)


def repack_in_proj(w_in, lay, tk=256):
    depth, d, n_src = w_in.shape
    assert n_src == lay.n_src
    w_in = w_in.astype(BF16)
    tail = jnp.pad(w_in[:, :, lay.tail_start:], ((0, 0), (0, 0), (0, LANE - (n_src - lay.tail_start))))
    return pl.pallas_call(
        functools.partial(_repack_kernel, lay=lay),
        grid=(depth, d // tk),
        in_specs=[pl.BlockSpec((None, tk, n_src), lambda l, i: (l, i, 0)),
                  pl.BlockSpec((None, tk, LANE), lambda l, i: (l, i, 0))],
        out_specs=pl.BlockSpec((None, tk, lay.n_dst), lambda l, i: (l, i, 0)),
        out_shape=jax.ShapeDtypeStruct((depth, d, lay.n_dst), BF16),
        compiler_params=_cparams("parallel", "parallel"),
        name="repack_in_proj",
    )(w_in, tail)


def _norm_matmul_kernel(x_ref, g_ref, w_ref, o_ref, u_ref):
    @pl.when(pl.program_id(1) == 0)
    def _():
        u_ref[...] = _rms_norm(x_ref[...], g_ref[...]).astype(BF16)

    o_ref[...] = _dot(u_ref[...], w_ref[...])


def norm_matmul(x, gain, w, layer, tm=1024, tn=1536):
    t, d = x.shape
    n = w.shape[2]
    return pl.pallas_call(
        _norm_matmul_kernel,
        grid=(t // tm, n // tn),
        in_specs=[pl.BlockSpec((tm, d), lambda i, j: (i, 0)),
                  pl.BlockSpec((None, 1, d), lambda i, j: (layer, 0, 0)),
                  pl.BlockSpec((None, d, tn), lambda i, j: (layer, 0, j))],
        out_specs=pl.BlockSpec((tm, tn), lambda i, j: (i, j)),
        out_shape=jax.ShapeDtypeStruct((t, n), F32),
        scratch_shapes=[pltpu.VMEM((tm, d), BF16)],
        compiler_params=_cparams("parallel", "arbitrary"),
        name="norm_in_proj",
    )(x, gain.reshape(gain.shape[0], 1, d), w)


def _pool_kernel(p_ref, w_ref, s_ref, o_ref):
    s, width = p_ref.shape
    group = width // len(POOL_WINDOWS)
    row = lax.broadcasted_iota(jnp.int32, (s, group), 0)
    n_seen = (row + 1).astype(F32)

    def shifted(x, k):
        return jnp.where(row >= k, pltpu.roll(x, k, 0), 0.0)

    for g, win in enumerate(POOL_WINDOWS):
        p = p_ref[:, g * group:(g + 1) * group]
        acc = p
        k = 1
        while k < win:
            acc = acc + shifted(acc, k)
            k *= 2
        d = acc / jnp.minimum(n_seen, float(win)) - p
        y = _dot(d.astype(BF16), w_ref[g].astype(BF16))
        o_ref[:, g * group:(g + 1) * group] = (y * s_ref[:, g * group:(g + 1) * group]).astype(o_ref.dtype)


def pool_mixer(proj, pool_w, pool_scale, layer, batch, seq, width, col):
    t = batch * seq
    _, n_g, group, _ = pool_w.shape
    col_blk = _block_index(col, width)
    return pl.pallas_call(
        _pool_kernel,
        grid=(batch,),
        in_specs=[pl.BlockSpec((seq, width), lambda b: (b, col_blk)),
                  pl.BlockSpec((None, n_g, group, group), lambda b: (layer, 0, 0, 0)),
                  pl.BlockSpec((None, 1, width), lambda b: (layer, 0, 0))],
        out_specs=pl.BlockSpec((seq, width), lambda b: (b, 0)),
        out_shape=jax.ShapeDtypeStruct((t, width), BF16),
        compiler_params=_cparams("parallel"),
        name="pool_mixer",
    )(proj, pool_w, pool_scale.reshape(pool_scale.shape[0], 1, width))


def _sb_kernel(q_ref, k_ref, v_ref, o_ref, *, blk, scale, heads):
    qi = pl.program_id(2)
    r = lax.broadcasted_iota(jnp.int32, (blk, blk), 0)
    c = lax.broadcasted_iota(jnp.int32, (blk, blk), 1)
    later_mat = (r > c).astype(BF16)
    causal = c < r
    qs = [(q_ref[:, h * HEAD_DIM:(h + 1) * HEAD_DIM] * -scale).astype(BF16) for h in range(heads)]

    def block(j, carry, masked):
        start = pl.multiple_of(j * blk, blk)
        hd = [slice(h * HEAD_DIM, (h + 1) * HEAD_DIM) for h in range(heads)]
        nzs = [_dot_nt(qs[h], k_ref[pl.ds(start, blk), hd[h]].astype(BF16)) for h in range(heads)]
        stay, his, los = [], [], []
        for nz in nzs:
            log_stay = jnp.minimum(nz, 0.0) - jnp.log(1.0 + jnp.exp(-jnp.abs(nz)))
            if masked:
                log_stay = jnp.where(causal, log_stay, 0.0)
            hi = log_stay.astype(BF16)
            stay.append(log_stay)
            his.append(hi)
            los.append((log_stay - hi.astype(F32)).astype(BF16))
        laters = [_dot(his[h], later_mat) + _dot(los[h], later_mat) for h in range(heads)]
        accs, runs = [], []
        for h in range(heads):
            a = jnp.exp(stay[h] - nzs[h] + (laters[h] + carry[h][1]))
            if masked:
                a = jnp.where(causal, a, 0.0)
            accs.append(carry[h][0] + _dot(a.astype(BF16), v_ref[pl.ds(start, blk), hd[h]].astype(BF16)))
            runs.append(carry[h][1] + (laters[h][:, 0:1] + stay[h][:, 0:1]))
        return tuple(zip(accs, runs))

    zero = (jnp.zeros((blk, HEAD_DIM), F32), jnp.zeros((blk, 1), F32))
    carry = block(qi, (zero,) * heads, True)
    carry = lax.fori_loop(0, qi, lambda i, cr: block(qi - 1 - i, cr, False), carry)
    for h in range(heads):
        o_ref[:, h * HEAD_DIM:(h + 1) * HEAD_DIM] = carry[h][0].astype(o_ref.dtype)


def sb_attention(proj, batch, seq, q_col, k_col, v_col, blk=256, heads=N_HEADS):
    t = batch * seq
    nq = seq // blk
    width = heads * HEAD_DIM
    qb, kb, vb = (_block_index(col, width) for col in (q_col, k_col, v_col))
    kernel = functools.partial(_sb_kernel, blk=blk, scale=HEAD_DIM ** -0.5, heads=heads)
    return pl.pallas_call(
        kernel,
        grid=(batch, N_HEADS // heads, nq),
        in_specs=[pl.BlockSpec((blk, width), lambda b, h, i: (b * nq + i, qb + h)),
                  pl.BlockSpec((seq, width), lambda b, h, i: (b, kb + h)),
                  pl.BlockSpec((seq, width), lambda b, h, i: (b, vb + h))],
        out_specs=pl.BlockSpec((blk, width), lambda b, h, i: (b * nq + i, h)),
        out_shape=jax.ShapeDtypeStruct((t, N_HEADS * HEAD_DIM), BF16),
        compiler_params=_cparams("parallel", "parallel", "arbitrary"),
        name="sb_attention",
    )(proj, proj, proj)


def _gdn_kernel(qkv_ref, z_ref, ab_ref, conv_ref, prm_ref, gain_ref, o_ref,
                xs_ref, q_s, k_s, v_s, u_s, o_s, wq_s, qk_s, kdt_s, dec_s, state_ref, *, ts):
    C = GDN_CHUNK
    W = N_HEADS * HEAD_DIM
    n_chunks = ts // C
    n_pairs = N_HEADS // 2
    pad = 8

    @pl.when(pl.program_id(1) == 0)
    def _():
        xs_ref[0:pad, :] = jnp.zeros((pad, 3 * W), F32)
        state_ref[...] = jnp.zeros_like(state_ref)

    xs_ref[pad:pad + ts, :] = qkv_ref[...]
    y = jnp.zeros((ts, 3 * W), F32)
    for i in range(GDN_CONV):
        off = pad - (GDN_CONV - 1) + i
        y = y + xs_ref[off:off + ts, :] * conv_ref[i:i + 1, :]
    xs_ref[0:pad, :] = xs_ref[ts:ts + pad, :]
    y = y * jax.nn.sigmoid(y)
    for h in range(N_HEADS):
        hd = slice(h * HEAD_DIM, (h + 1) * HEAD_DIM)
        qh = y[:, hd]
        kh = y[:, W + h * HEAD_DIM:W + (h + 1) * HEAD_DIM]
        q_s[:, hd] = qh * lax.rsqrt(jnp.sum(qh * qh, axis=-1, keepdims=True) + EPS) * (HEAD_DIM ** -0.5)
        k_s[:, hd] = kh * lax.rsqrt(jnp.sum(kh * kh, axis=-1, keepdims=True) + EPS)
    v_s[...] = y[:, 2 * W:]

    ab = ab_ref[...]
    log_alpha = -jnp.exp(prm_ref[0:1, :]) * _softplus(ab + prm_ref[1:2, :])
    rt = lax.broadcasted_iota(jnp.int32, (ts, ts), 0)
    ct = lax.broadcasted_iota(jnp.int32, (ts, ts), 1)
    chunk_tril = ((rt // C == ct // C) & (ct <= rt)).astype(BF16)
    a1, a2, a3 = _split3(log_alpha)
    g_all = _dot(chunk_tril, a1) + _dot(chunk_tril, a2) + _dot(chunk_tril, a3)
    beta_all = jax.nn.sigmoid(ab)

    r2 = lax.broadcasted_iota(jnp.int32, (2 * C, 2 * C), 0)
    c2 = lax.broadcasted_iota(jnp.int32, (2 * C, 2 * C), 1)
    same = (r2 < C) == (c2 < C)
    incl = same & (c2 <= r2)
    strict = same & (c2 < r2)
    eye = (r2 == c2).astype(F32)
    row_first = r2 < C
    lane_first = c2 < C
    lane_first_row = lax.broadcasted_iota(jnp.int32, (1, 2 * C), 1) < C

    hd = [slice(h * HEAD_DIM, (h + 1) * HEAD_DIM) for h in range(N_HEADS)]
    inst = [(ci, p) for ci in range(n_chunks) for p in range(n_pairs)]
    g_ts = []
    for ci in range(n_chunks):
        g_c = g_all[ci * C:(ci + 1) * C]
        g_t = jnp.concatenate([g_c, g_c], axis=0).T
        dec_s[ci] = jnp.broadcast_to(jnp.exp(g_t[0:8, C - 1:C]), (8, LANE))
        g_ts.append(g_t)

    def pair_rows(ref, ci, p):
        rows = slice(ci * C, (ci + 1) * C)
        return jnp.concatenate([ref[rows, hd[2 * p]], ref[rows, hd[2 * p + 1]]], axis=0)

    def col_bcast(x, ci, p, lane0):
        x = x[ci * C:(ci + 1) * C]
        return jnp.concatenate(
            [jnp.broadcast_to(x[:, lane0 + h:lane0 + h + 1], (C, 2 * C)) for h in (2 * p, 2 * p + 1)], axis=0)

    gb = [col_bcast(g_all, ci, p, 0) for ci, p in inst]
    bb = [col_bcast(beta_all, ci, p, N_HEADS) for ci, p in inst]
    gr = [jnp.where(lane_first_row, g_ts[ci][2 * p:2 * p + 1, :], g_ts[ci][2 * p + 1:2 * p + 2, :])
          for ci, p in inst]
    gamma = [jnp.where(incl, jnp.exp(jnp.where(incl, gb[i] - gr[i], 0.0)), 0.0) for i in range(len(inst))]
    k2t = [pair_rows(k_s, ci, p).T for ci, p in inst]
    kq = [_dot(jnp.concatenate([pair_rows(k_s, ci, p) * bb[i], pair_rows(q_s, ci, p)], axis=0).astype(BF16),
               k2t[i].astype(BF16)) for i, (ci, p) in enumerate(inst)]
    pw = [jnp.where(strict, kq[i][:2 * C] * gamma[i], 0.0) for i in range(len(inst))]
    inv = [eye - x for x in pw]
    for _ in range(5):
        pw = [_dot(x.astype(BF16), x.astype(BF16)) for x in pw]
        inv = [y + _dot(y.astype(BF16), x.astype(BF16)) for y, x in zip(inv, pw)]
    eg = [jnp.exp(x) for x in gb]
    rhs = [jnp.concatenate([pair_rows(v_s, ci, p) * bb[i], pair_rows(k_s, ci, p) * (bb[i] * eg[i])], axis=1)
           for i, (ci, p) in enumerate(inst)]
    sol = [rhs[i] + _dot((inv[i] - eye).astype(BF16), rhs[i].astype(BF16)) for i in range(len(inst))]
    for i, (ci, p) in enumerate(inst):
        rows = slice(ci * C, (ci + 1) * C)
        w2 = sol[i][:, HEAD_DIM:].astype(BF16)
        qd2 = (pair_rows(q_s, ci, p) * eg[i]).astype(BF16)
        qk_s[ci, p] = (kq[i][2 * C:] * gamma[i]).astype(BF16)
        g_last = jnp.where(lane_first_row, g_ts[ci][2 * p:2 * p + 1, C - 1:C], g_ts[ci][2 * p + 1:2 * p + 2, C - 1:C])
        kdt = k2t[i] * jnp.exp(g_last - gr[i])
        for n, h in enumerate((2 * p, 2 * p + 1)):
            half = slice(n * C, (n + 1) * C)
            u_s[rows, hd[h]] = sol[i][half, :HEAD_DIM]
            wq_s[ci, h] = jnp.concatenate([w2[half], qd2[half]], axis=0)
            kdt_s[ci, h] = jnp.where(lane_first if n == 0 else ~lane_first, kdt, 0.0).astype(BF16)

    def chunk(ci, _):
        rows = pl.ds(pl.multiple_of(ci * C, C), C)
        dec = dec_s[ci]
        heads = range(N_HEADS)
        rs = [_dot(wq_s[ci, h], state_ref[h].astype(BF16)) for h in heads]
        vn2 = [jnp.concatenate([u_s[rows, hd[h]] - rs[h][:C] for h in (2 * p, 2 * p + 1)], axis=0).astype(BF16)
               for p in range(n_pairs)]
        o2 = [jnp.concatenate([rs[2 * p][C:], rs[2 * p + 1][C:]], axis=0) + _dot(qk_s[ci, p], vn2[p])
              for p in range(n_pairs)]
        new_state = [state_ref[h] * dec[h:h + 1, :] + _dot(kdt_s[ci, h], vn2[h // 2]) for h in heads]
        for h in heads:
            state_ref[h] = new_state[h]
            o_s[rows, hd[h]] = o2[h // 2][(h % 2) * C:(h % 2 + 1) * C]
        return 0

    lax.fori_loop(0, n_chunks, chunk, 0)

    gain = gain_ref[...]
    for h in range(N_HEADS):
        hd = slice(h * HEAD_DIM, (h + 1) * HEAD_DIM)
        zh = z_ref[:, hd]
        o_ref[:, hd] = (_rms_norm(o_s[:, hd], gain) * (zh * jax.nn.sigmoid(zh))).astype(o_ref.dtype)


def gdn_mixer(proj, conv_w, prm, norm_gain, layer, batch, seq, qkv_col, z_col, ab_col, ts=256):
    t = batch * seq
    W = N_HEADS * HEAD_DIM
    ns = seq // ts
    nc = ts // GDN_CHUNK
    qkv_blk = _block_index(qkv_col, 3 * W)
    z_blk = _block_index(z_col, W)
    kernel = functools.partial(_gdn_kernel, ts=ts)
    return pl.pallas_call(
        kernel,
        grid=(batch, ns),
        in_specs=[pl.BlockSpec((ts, 3 * W), lambda b, s: (b * ns + s, qkv_blk)),
                  pl.BlockSpec((ts, W), lambda b, s: (b * ns + s, z_blk)),
                  pl.BlockSpec((ts, LANE), lambda b, s: (b * ns + s, ab_col)),
                  pl.BlockSpec((None, GDN_CONV, 3 * W), lambda b, s: (layer, 0, 0)),
                  pl.BlockSpec((None, 8, LANE), lambda b, s: (layer, 0, 0)),
                  pl.BlockSpec((None, 1, HEAD_DIM), lambda b, s: (layer, 0, 0))],
        out_specs=pl.BlockSpec((ts, W), lambda b, s: (b * ns + s, 0)),
        out_shape=jax.ShapeDtypeStruct((t, W), BF16),
        scratch_shapes=[pltpu.VMEM((ts + 8, 3 * W), F32),
                        pltpu.VMEM((ts, W), F32), pltpu.VMEM((ts, W), F32),
                        pltpu.VMEM((ts, W), F32), pltpu.VMEM((ts, W), F32),
                        pltpu.VMEM((ts, W), F32),
                        pltpu.VMEM((nc, N_HEADS, 2 * GDN_CHUNK, HEAD_DIM), BF16),
                        pltpu.VMEM((nc, N_HEADS // 2, 2 * GDN_CHUNK, 2 * GDN_CHUNK), BF16),
                        pltpu.VMEM((nc, N_HEADS, HEAD_DIM, 2 * GDN_CHUNK), BF16),
                        pltpu.VMEM((nc, 8, LANE), F32),
                        pltpu.VMEM((N_HEADS, HEAD_DIM, HEAD_DIM), F32)],
        compiler_params=_cparams("parallel", "arbitrary"),
        name="gdn_mixer",
    )(proj, proj, proj, conv_w, prm, norm_gain.reshape(norm_gain.shape[0], 1, HEAD_DIM))


def _merge_kernel(yp_ref, ys_ref, yg_ref, gp_ref, gs_ref, gg_ref, wp_ref, ws_ref, wg_ref, o_ref):
    m = jax.nn.sigmoid(gp_ref[...]) * _dot(yp_ref[...], wp_ref[...])
    m = m + jax.nn.sigmoid(gs_ref[...]) * _dot(ys_ref[...], ws_ref[...])
    m = m + jax.nn.sigmoid(gg_ref[...]) * _dot(yg_ref[...], wg_ref[...])
    o_ref[...] = m.astype(o_ref.dtype)


def merge_branches(y_pool, y_sb, y_gdn, proj, w_pool_up, w_sb_up, w_gdn_up, layer, gate_col, tm=1024, tn=1024):
    t = y_pool.shape[0]
    d = w_pool_up.shape[2]
    nb = d // tn
    g0 = _block_index(gate_col, tn)

    def y_spec(y):
        return pl.BlockSpec((tm, y.shape[1]), lambda i, j: (i, 0))

    def w_spec(w):
        return pl.BlockSpec((None, w.shape[1], tn), lambda i, j: (layer, 0, j))

    def gate_spec(branch):
        return pl.BlockSpec((tm, tn), lambda i, j: (i, g0 + branch * nb + j))

    return pl.pallas_call(
        _merge_kernel,
        grid=(t // tm, nb),
        in_specs=[y_spec(y_pool), y_spec(y_sb), y_spec(y_gdn),
                  gate_spec(0), gate_spec(1), gate_spec(2),
                  w_spec(w_pool_up), w_spec(w_sb_up), w_spec(w_gdn_up)],
        out_specs=pl.BlockSpec((tm, tn), lambda i, j: (i, j)),
        out_shape=jax.ShapeDtypeStruct((t, d), BF16),
        compiler_params=_cparams("parallel", "arbitrary"),
        name="merge_branches",
    )(y_pool, y_sb, y_gdn, proj, proj, proj, w_pool_up, w_sb_up, w_gdn_up)


def _matmul_res_kernel(a_ref, w_ref, r_ref, o_ref):
    o_ref[...] = r_ref[...] + _dot(a_ref[...], w_ref[...])


def matmul_residual(a, w, res, layer, tm=1024, tn=1024):
    t, k = a.shape
    n = w.shape[2]
    return pl.pallas_call(
        _matmul_res_kernel,
        grid=(t // tm, n // tn),
        in_specs=[pl.BlockSpec((tm, k), lambda i, j: (i, 0)),
                  pl.BlockSpec((None, k, tn), lambda i, j: (layer, 0, j)),
                  pl.BlockSpec((tm, tn), lambda i, j: (i, j))],
        out_specs=pl.BlockSpec((tm, tn), lambda i, j: (i, j)),
        out_shape=jax.ShapeDtypeStruct((t, n), F32),
        compiler_params=_cparams("parallel", "arbitrary"),
        name="out_proj_residual",
    )(a, w, res)


def _mlp_kernel(x_ref, g_ref, w1_ref, w2_ref, fg_ref, o_ref, u_ref, acc_ref, *, final_norm):
    j = pl.program_id(1)

    @pl.when(j == 0)
    def _():
        u_ref[...] = _rms_norm(x_ref[...], g_ref[...]).astype(BF16)
        acc_ref[...] = jnp.zeros_like(acc_ref)

    h = jnp.maximum(_dot(u_ref[...], w1_ref[...]), 0.0)
    acc_ref[...] += _dot((h * h).astype(BF16), w2_ref[...])

    @pl.when(j == pl.num_programs(1) - 1)
    def _():
        out = x_ref[...] + acc_ref[...]
        if final_norm:
            out = _rms_norm(out, fg_ref[...])
        o_ref[...] = out


def mlp_block(x, gain, w1, w2, final_gain, layer, final_norm, tm=512, tf=1024):
    t, d = x.shape
    f = w1.shape[2]
    kernel = functools.partial(_mlp_kernel, final_norm=final_norm)
    return pl.pallas_call(
        kernel,
        grid=(t // tm, f // tf),
        in_specs=[pl.BlockSpec((tm, d), lambda i, j: (i, 0)),
                  pl.BlockSpec((None, 1, d), lambda i, j: (layer, 0, 0)),
                  pl.BlockSpec((None, d, tf), lambda i, j: (layer, 0, j)),
                  pl.BlockSpec((None, tf, d), lambda i, j: (layer, j, 0)),
                  pl.BlockSpec((1, d), lambda i, j: (0, 0))],
        out_specs=pl.BlockSpec((tm, d), lambda i, j: (i, 0)),
        out_shape=jax.ShapeDtypeStruct((t, d), F32),
        scratch_shapes=[pltpu.VMEM((tm, d), BF16), pltpu.VMEM((tm, d), F32)],
        compiler_params=_cparams("parallel", "arbitrary"),
        name="mlp_block",
    )(x, gain.reshape(gain.shape[0], 1, d), w1, w2, final_gain.reshape(1, d))


def kernel(x, attn_norm, w_in, pool_w, pool_scale, gdn_conv, gdn_a_log, gdn_dt_bias, gdn_norm,
           w_pool_up, w_sb_up, w_gdn_up, w_out, mlp_norm, w_ff1, w_ff2, final_norm):
    batch, seq, d = x.shape
    depth = w_in.shape[0]
    pool_width = pool_w.shape[1] * pool_w.shape[2]
    lay = InProjLayout(pool_width, d)

    w_in_p = repack_in_proj(w_in, lay)
    w_pool_up, w_sb_up, w_gdn_up, w_out, w_ff1, w_ff2 = (
        cast_bf16(w) for w in (w_pool_up, w_sb_up, w_gdn_up, w_out, w_ff1, w_ff2))
    gdn_prm = jnp.pad(jnp.stack([gdn_a_log, gdn_dt_bias], axis=1), ((0, 0), (0, 6), (0, LANE - N_HEADS)))

    xt = x.reshape(batch * seq, d)
    for l in range(depth):
        proj = norm_matmul(xt, attn_norm, w_in_p, l)
        y_pool = pool_mixer(proj, pool_w, pool_scale, l, batch, seq, pool_width, lay.dst_pool)
        y_sb = sb_attention(proj, batch, seq, lay.dst_sb, lay.dst_sb + N_HEADS, lay.dst_sb + 2 * N_HEADS)
        y_gdn = gdn_mixer(proj, gdn_conv, gdn_prm, gdn_norm, l, batch, seq, lay.dst_gdn, lay.dst_z, lay.dst_ab)
        merged = merge_branches(y_pool, y_sb, y_gdn, proj, w_pool_up, w_sb_up, w_gdn_up, l, lay.dst_gate)
        xt = matmul_residual(merged, w_out, xt, l)
        xt = mlp_block(xt, mlp_norm, w_ff1, w_ff2, final_norm, l, final_norm=(l == depth - 1))
    return xt.reshape(batch, seq, d)
```

```python
import functools

import jax
import jax.numpy as jnp
from jax import lax
from jax.experimental import pallas as pl
from jax.experimental.pallas import tpu as pltpu

EPS = 1e-6
LANE = 128
VMEM_LIMIT = 56 * 1024 * 1024

POOL_WINDOWS = (2, 4, 8, 16)
HEAD_DIM = 128
N_HEADS = 6
GDN_CONV = 4
GDN_CHUNK = 64

BF16 = jnp.bfloat16
F32 = jnp.float32


def _cparams(*sem):
    return pltpu.CompilerParams(dimension_semantics=sem, vmem_limit_bytes=VMEM_LIMIT)


def _rms_norm(x, gain):
    return x * lax.rsqrt(jnp.mean(x * x, axis=-1, keepdims=True) + EPS) * gain


def _softplus(x):
    return jnp.maximum(x, 0.0) + jnp.log1p(jnp.exp(-jnp.abs(x)))


def _split3(x):
    h1 = x.astype(BF16)
    r1 = x - h1.astype(F32)
    h2 = r1.astype(BF16)
    h3 = (r1 - h2.astype(F32)).astype(BF16)
    return h1, h2, h3


def _block_index(lane_col, width):
    assert (lane_col * LANE) % width == 0, (lane_col, width)
    return lane_col * LANE // width


def _dot(a, b):
    return jnp.dot(a, b, preferred_element_type=F32)


def _dot_nt(a, b):
    return lax.dot_general(a, b, (((1,), (1,)), ((), ())), preferred_element_type=F32)


CAST_BLOCK_BYTES = 8 * 1024 * 1024


def _cast_kernel(w_ref, o_ref):
    o_ref[...] = w_ref[...].astype(o_ref.dtype)


def cast_bf16(w):
    depth, rows, cols = w.shape
    tr = min(rows, CAST_BLOCK_BYTES // (cols * 4))
    assert rows % tr == 0
    return pl.pallas_call(
        _cast_kernel,
        grid=(depth, rows // tr),
        in_specs=[pl.BlockSpec((None, tr, cols), lambda l, i: (l, i, 0))],
        out_specs=pl.BlockSpec((None, tr, cols), lambda l, i: (l, i, 0)),
        out_shape=jax.ShapeDtypeStruct(w.shape, BF16),
        compiler_params=_cparams("parallel", "parallel"),
        name="cast_bf16",
    )(w)


class InProjLayout:
    def __init__(self, pool_width, d_model):
        w = N_HEADS * HEAD_DIM
        self.n_ab = 2 * N_HEADS
        self.src_pool, self.src_sb, self.src_gdn = 0, pool_width, pool_width + 3 * w
        self.src_z = self.src_gdn + 3 * w
        self.src_ab = self.src_z + w
        self.src_gate = self.src_ab + self.n_ab
        self.n_gate = 3 * d_model
        self.n_src = self.src_gate + self.n_gate
        self.ab_pad = 2 * LANE
        self.dst_gdn = 0
        self.dst_z = self.dst_gdn + 3 * N_HEADS
        self.dst_sb = self.dst_z + N_HEADS
        self.dst_ab = self.dst_sb + 3 * N_HEADS
        self.dst_pool = self.dst_ab + self.ab_pad // LANE
        self.dst_gate = self.dst_pool + pool_width // LANE
        self.n_dst = self.dst_gate * LANE + self.n_gate
        self.shift = self.src_gate % LANE
        self.gate_base = self.src_gate - self.shift
        assert self.src_ab % LANE == 0 and self.gate_base == self.src_ab and self.shift == self.n_ab
        self.tail_start = (self.n_src // LANE) * LANE


def _repack_kernel(w_ref, tail_ref, o_ref, *, lay):
    def copy(dst_col, src, n):
        o_ref[:, dst_col * LANE:dst_col * LANE + n] = w_ref[:, src:src + n]

    w = N_HEADS * HEAD_DIM
    copy(lay.dst_gdn, lay.src_gdn, 3 * w)
    copy(lay.dst_z, lay.src_z, w)
    copy(lay.dst_pool, lay.src_pool, lay.src_sb - lay.src_pool)
    copy(lay.dst_sb, lay.src_sb, 3 * w)
    rows = w_ref.shape[0]
    lane = lax.broadcasted_iota(jnp.int32, (rows, LANE), 1)
    ab0 = lay.dst_ab * LANE
    o_ref[:, ab0:ab0 + LANE] = jnp.where(lane < lay.n_ab, w_ref[:, lay.src_ab:lay.src_ab + LANE].astype(F32), 0.0).astype(BF16)
    o_ref[:, ab0 + LANE:ab0 + lay.ab_pad] = jnp.zeros((rows, lay.ab_pad - LANE), BF16)

    def src_block(col):
        if col + LANE <= lay.tail_start:
            return w_ref[:, col:col + LANE].astype(F32)
        assert col == lay.tail_start
        return tail_ref[...].astype(F32)

    for c in range(lay.n_gate // LANE):
        col = lay.gate_base + c * LANE
        a = pltpu.roll(src_block(col), LANE - lay.shift, 1)
        b = pltpu.roll(src_block(col + LANE), LANE - lay.shift, 1)
        g0 = lay.dst_gate * LANE + c * LANE
        o_ref[:, g0:g0 + LANE] = jnp.where(lane < LANE - lay.shift, a, b).astype(BF16)


def repack_in_proj(w_in, lay, tk=256):
    depth, d, n_src = w_in.shape
    assert n_src == lay.n_src
    w_in = w_in.astype(BF16)
    tail = jnp.pad(w_in[:, :, lay.tail_start:], ((0, 0), (0, 0), (0, LANE - (n_src - lay.tail_start))))
    return pl.pallas_call(
        functools.partial(_repack_kernel, lay=lay),
        grid=(depth, d // tk),
        in_specs=[pl.BlockSpec((None, tk, n_src), lambda l, i: (l, i, 0)),
                  pl.BlockSpec((None, tk, LANE), lambda l, i: (l, i, 0))],
        out_specs=pl.BlockSpec((None, tk, lay.n_dst), lambda l, i: (l, i, 0)),
        out_shape=jax.ShapeDtypeStruct((depth, d, lay.n_dst), BF16),
        compiler_params=_cparams("parallel", "parallel"),
        name="repack_in_proj",
    )(w_in, tail)


def _norm_matmul_kernel(x_ref, g_ref, w_ref, o_ref, u_ref):
    @pl.when(pl.program_id(1) == 0)
    def _():
        u_ref[...] = _rms_norm(x_ref[...], g_ref[...]).astype(BF16)

    o_ref[...] = _dot(u_ref[...], w_ref[...])


def norm_matmul(x, gain, w, layer, tm=1024, tn=1536):
    t, d = x.shape
    n = w.shape[2]
    return pl.pallas_call(
        _norm_matmul_kernel,
        grid=(t // tm, n // tn),
        in_specs=[pl.BlockSpec((tm, d), lambda i, j: (i, 0)),
                  pl.BlockSpec((None, 1, d), lambda i, j: (layer, 0, 0)),
                  pl.BlockSpec((None, d, tn), lambda i, j: (layer, 0, j))],
        out_specs=pl.BlockSpec((tm, tn), lambda i, j: (i, j)),
        out_shape=jax.ShapeDtypeStruct((t, n), F32),
        scratch_shapes=[pltpu.VMEM((tm, d), BF16)],
        compiler_params=_cparams("parallel", "arbitrary"),
        name="norm_in_proj",
    )(x, gain.reshape(gain.shape[0], 1, d), w)


def _pool_kernel(p_ref, w_ref, s_ref, o_ref):
    s, width = p_ref.shape
    group = width // len(POOL_WINDOWS)
    row = lax.broadcasted_iota(jnp.int32, (s, group), 0)
    n_seen = (row + 1).astype(F32)

    def shifted(x, k):
        return jnp.where(row >= k, pltpu.roll(x, k, 0), 0.0)

    for g, win in enumerate(POOL_WINDOWS):
        p = p_ref[:, g * group:(g + 1) * group]
        acc = p
        k = 1
        while k < win:
            acc = acc + shifted(acc, k)
            k *= 2
        d = acc / jnp.minimum(n_seen, float(win)) - p
        y = _dot(d.astype(BF16), w_ref[g].astype(BF16))
        o_ref[:, g * group:(g + 1) * group] = (y * s_ref[:, g * group:(g + 1) * group]).astype(o_ref.dtype)


def pool_mixer(proj, pool_w, pool_scale, layer, batch, seq, width, col):
    t = batch * seq
    _, n_g, group, _ = pool_w.shape
    col_blk = _block_index(col, width)
    return pl.pallas_call(
        _pool_kernel,
        grid=(batch,),
        in_specs=[pl.BlockSpec((seq, width), lambda b: (b, col_blk)),
                  pl.BlockSpec((None, n_g, group, group), lambda b: (layer, 0, 0, 0)),
                  pl.BlockSpec((None, 1, width), lambda b: (layer, 0, 0))],
        out_specs=pl.BlockSpec((seq, width), lambda b: (b, 0)),
        out_shape=jax.ShapeDtypeStruct((t, width), BF16),
        compiler_params=_cparams("parallel"),
        name="pool_mixer",
    )(proj, pool_w, pool_scale.reshape(pool_scale.shape[0], 1, width))


def _sb_kernel(q_ref, k_ref, v_ref, o_ref, *, blk, scale, heads):
    qi = pl.program_id(2)
    r = lax.broadcasted_iota(jnp.int32, (blk, blk), 0)
    c = lax.broadcasted_iota(jnp.int32, (blk, blk), 1)
    later_mat = (r > c).astype(BF16)
    causal = c < r
    qs = [(q_ref[:, h * HEAD_DIM:(h + 1) * HEAD_DIM] * -scale).astype(BF16) for h in range(heads)]

    def block(j, carry, masked):
        start = pl.multiple_of(j * blk, blk)
        hd = [slice(h * HEAD_DIM, (h + 1) * HEAD_DIM) for h in range(heads)]
        nzs = [_dot_nt(qs[h], k_ref[pl.ds(start, blk), hd[h]].astype(BF16)) for h in range(heads)]
        stay, his, los = [], [], []
        for nz in nzs:
            log_stay = jnp.minimum(nz, 0.0) - jnp.log(1.0 + jnp.exp(-jnp.abs(nz)))
            if masked:
                log_stay = jnp.where(causal, log_stay, 0.0)
            hi = log_stay.astype(BF16)
            stay.append(log_stay)
            his.append(hi)
            los.append((log_stay - hi.astype(F32)).astype(BF16))
        laters = [_dot(his[h], later_mat) + _dot(los[h], later_mat) for h in range(heads)]
        accs, runs = [], []
        for h in range(heads):
            a = jnp.exp(stay[h] - nzs[h] + (laters[h] + carry[h][1]))
            if masked:
                a = jnp.where(causal, a, 0.0)
            accs.append(carry[h][0] + _dot(a.astype(BF16), v_ref[pl.ds(start, blk), hd[h]].astype(BF16)))
            runs.append(carry[h][1] + (laters[h][:, 0:1] + stay[h][:, 0:1]))
        return tuple(zip(accs, runs))

    zero = (jnp.zeros((blk, HEAD_DIM), F32), jnp.zeros((blk, 1), F32))
    carry = block(qi, (zero,) * heads, True)
    carry = lax.fori_loop(0, qi, lambda i, cr: block(qi - 1 - i, cr, False), carry)
    for h in range(heads):
        o_ref[:, h * HEAD_DIM:(h + 1) * HEAD_DIM] = carry[h][0].astype(o_ref.dtype)


def sb_attention(proj, batch, seq, q_col, k_col, v_col, blk=256, heads=N_HEADS):
    t = batch * seq
    nq = seq // blk
    width = heads * HEAD_DIM
    qb, kb, vb = (_block_index(col, width) for col in (q_col, k_col, v_col))
    kernel = functools.partial(_sb_kernel, blk=blk, scale=HEAD_DIM ** -0.5, heads=heads)
    return pl.pallas_call(
        kernel,
        grid=(batch, N_HEADS // heads, nq),
        in_specs=[pl.BlockSpec((blk, width), lambda b, h, i: (b * nq + i, qb + h)),
                  pl.BlockSpec((seq, width), lambda b, h, i: (b, kb + h)),
                  pl.BlockSpec((seq, width), lambda b, h, i: (b, vb + h))],
        out_specs=pl.BlockSpec((blk, width), lambda b, h, i: (b * nq + i, h)),
        out_shape=jax.ShapeDtypeStruct((t, N_HEADS * HEAD_DIM), BF16),
        compiler_params=_cparams("parallel", "parallel", "arbitrary"),
        name="sb_attention",
    )(proj, proj, proj)


def _gdn_kernel(qkv_ref, z_ref, ab_ref, conv_ref, prm_ref, gain_ref, o_ref,
                xs_ref, q_s, k_s, v_s, u_s, o_s, wq_s, qk_s, kdt_s, dec_s, state_ref, *, ts):
    C = GDN_CHUNK
    W = N_HEADS * HEAD_DIM
    n_chunks = ts // C
    n_pairs = N_HEADS // 2
    pad = 8

    @pl.when(pl.program_id(1) == 0)
    def _():
        xs_ref[0:pad, :] = jnp.zeros((pad, 3 * W), F32)
        state_ref[...] = jnp.zeros_like(state_ref)

    xs_ref[pad:pad + ts, :] = qkv_ref[...]
    y = jnp.zeros((ts, 3 * W), F32)
    for i in range(GDN_CONV):
        off = pad - (GDN_CONV - 1) + i
        y = y + xs_ref[off:off + ts, :] * conv_ref[i:i + 1, :]
    xs_ref[0:pad, :] = xs_ref[ts:ts + pad, :]
    y = y * jax.nn.sigmoid(y)
    for h in range(N_HEADS):
        hd = slice(h * HEAD_DIM, (h + 1) * HEAD_DIM)
        qh = y[:, hd]
        kh = y[:, W + h * HEAD_DIM:W + (h + 1) * HEAD_DIM]
        q_s[:, hd] = qh * lax.rsqrt(jnp.sum(qh * qh, axis=-1, keepdims=True) + EPS) * (HEAD_DIM ** -0.5)
        k_s[:, hd] = kh * lax.rsqrt(jnp.sum(kh * kh, axis=-1, keepdims=True) + EPS)
    v_s[...] = y[:, 2 * W:]

    ab = ab_ref[...]
    log_alpha = -jnp.exp(prm_ref[0:1, :]) * _softplus(ab + prm_ref[1:2, :])
    rt = lax.broadcasted_iota(jnp.int32, (ts, ts), 0)
    ct = lax.broadcasted_iota(jnp.int32, (ts, ts), 1)
    chunk_tril = ((rt // C == ct // C) & (ct <= rt)).astype(BF16)
    a1, a2, a3 = _split3(log_alpha)
    g_all = _dot(chunk_tril, a1) + _dot(chunk_tril, a2) + _dot(chunk_tril, a3)
    beta_all = jax.nn.sigmoid(ab)

    r2 = lax.broadcasted_iota(jnp.int32, (2 * C, 2 * C), 0)
    c2 = lax.broadcasted_iota(jnp.int32, (2 * C, 2 * C), 1)
    same = (r2 < C) == (c2 < C)
    incl = same & (c2 <= r2)
    strict = same & (c2 < r2)
    eye = (r2 == c2).astype(F32)
    row_first = r2 < C
    lane_first = c2 < C
    lane_first_row = lax.broadcasted_iota(jnp.int32, (1, 2 * C), 1) < C

    hd = [slice(h * HEAD_DIM, (h + 1) * HEAD_DIM) for h in range(N_HEADS)]
    inst = [(ci, p) for ci in range(n_chunks) for p in range(n_pairs)]
    g_ts = []
    for ci in range(n_chunks):
        g_c = g_all[ci * C:(ci + 1) * C]
        g_t = jnp.concatenate([g_c, g_c], axis=0).T
        dec_s[ci] = jnp.broadcast_to(jnp.exp(g_t[0:8, C - 1:C]), (8, LANE))
        g_ts.append(g_t)

    def pair_rows(ref, ci, p):
        rows = slice(ci * C, (ci + 1) * C)
        return jnp.concatenate([ref[rows, hd[2 * p]], ref[rows, hd[2 * p + 1]]], axis=0)

    def col_bcast(x, ci, p, lane0):
        x = x[ci * C:(ci + 1) * C]
        return jnp.concatenate(
            [jnp.broadcast_to(x[:, lane0 + h:lane0 + h + 1], (C, 2 * C)) for h in (2 * p, 2 * p + 1)], axis=0)

    gb = [col_bcast(g_all, ci, p, 0) for ci, p in inst]
    bb = [col_bcast(beta_all, ci, p, N_HEADS) for ci, p in inst]
    gr = [jnp.where(lane_first_row, g_ts[ci][2 * p:2 * p + 1, :], g_ts[ci][2 * p + 1:2 * p + 2, :])
          for ci, p in inst]
    gamma = [jnp.where(incl, jnp.exp(jnp.where(incl, gb[i] - gr[i], 0.0)), 0.0) for i in range(len(inst))]
    k2t = [pair_rows(k_s, ci, p).T for ci, p in inst]
    kq = [_dot(jnp.concatenate([pair_rows(k_s, ci, p) * bb[i], pair_rows(q_s, ci, p)], axis=0).astype(BF16),
               k2t[i].astype(BF16)) for i, (ci, p) in enumerate(inst)]
    pw = [jnp.where(strict, kq[i][:2 * C] * gamma[i], 0.0) for i in range(len(inst))]
    inv = [eye - x for x in pw]
    for _ in range(5):
        pw = [_dot(x.astype(BF16), x.astype(BF16)) for x in pw]
        inv = [y + _dot(y.astype(BF16), x.astype(BF16)) for y, x in zip(inv, pw)]
    eg = [jnp.exp(x) for x in gb]
    rhs = [jnp.concatenate([pair_rows(v_s, ci, p) * bb[i], pair_rows(k_s, ci, p) * (bb[i] * eg[i])], axis=1)
           for i, (ci, p) in enumerate(inst)]
    sol = [rhs[i] + _dot((inv[i] - eye).astype(BF16), rhs[i].astype(BF16)) for i in range(len(inst))]
    for i, (ci, p) in enumerate(inst):
        rows = slice(ci * C, (ci + 1) * C)
        w2 = sol[i][:, HEAD_DIM:].astype(BF16)
        qd2 = (pair_rows(q_s, ci, p) * eg[i]).astype(BF16)
        qk_s[ci, p] = (kq[i][2 * C:] * gamma[i]).astype(BF16)
        g_last = jnp.where(lane_first_row, g_ts[ci][2 * p:2 * p + 1, C - 1:C], g_ts[ci][2 * p + 1:2 * p + 2, C - 1:C])
        kdt = k2t[i] * jnp.exp(g_last - gr[i])
        for n, h in enumerate((2 * p, 2 * p + 1)):
            half = slice(n * C, (n + 1) * C)
            u_s[rows, hd[h]] = sol[i][half, :HEAD_DIM]
            wq_s[ci, h] = jnp.concatenate([w2[half], qd2[half]], axis=0)
            kdt_s[ci, h] = jnp.where(lane_first if n == 0 else ~lane_first, kdt, 0.0).astype(BF16)

    def chunk(ci, _):
        rows = pl.ds(pl.multiple_of(ci * C, C), C)
        dec = dec_s[ci]
        heads = range(N_HEADS)
        rs = [_dot(wq_s[ci, h], state_ref[h].astype(BF16)) for h in heads]
        vn2 = [jnp.concatenate([u_s[rows, hd[h]] - rs[h][:C] for h in (2 * p, 2 * p + 1)], axis=0).astype(BF16)
               for p in range(n_pairs)]
        o2 = [jnp.concatenate([rs[2 * p][C:], rs[2 * p + 1][C:]], axis=0) + _dot(qk_s[ci, p], vn2[p])
              for p in range(n_pairs)]
        new_state = [state_ref[h] * dec[h:h + 1, :] + _dot(kdt_s[ci, h], vn2[h // 2]) for h in heads]
        for h in heads:
            state_ref[h] = new_state[h]
            o_s[rows, hd[h]] = o2[h // 2][(h % 2) * C:(h % 2 + 1) * C]
        return 0

    lax.fori_loop(0, n_chunks, chunk, 0)

    gain = gain_ref[...]
    for h in range(N_HEADS):
        hd = slice(h * HEAD_DIM, (h + 1) * HEAD_DIM)
        zh = z_ref[:, hd]
        o_ref[:, hd] = (_rms_norm(o_s[:, hd], gain) * (zh * jax.nn.sigmoid(zh))).astype(o_ref.dtype)


def gdn_mixer(proj, conv_w, prm, norm_gain, layer, batch, seq, qkv_col, z_col, ab_col, ts=512):
    t = batch * seq
    W = N_HEADS * HEAD_DIM
    ns = seq // ts
    nc = ts // GDN_CHUNK
    qkv_blk = _block_index(qkv_col, 3 * W)
    z_blk = _block_index(z_col, W)
    kernel = functools.partial(_gdn_kernel, ts=ts)
    return pl.pallas_call(
        kernel,
        grid=(batch, ns),
        in_specs=[pl.BlockSpec((ts, 3 * W), lambda b, s: (b * ns + s, qkv_blk)),
                  pl.BlockSpec((ts, W), lambda b, s: (b * ns + s, z_blk)),
                  pl.BlockSpec((ts, LANE), lambda b, s: (b * ns + s, ab_col)),
                  pl.BlockSpec((None, GDN_CONV, 3 * W), lambda b, s: (layer, 0, 0)),
                  pl.BlockSpec((None, 8, LANE), lambda b, s: (layer, 0, 0)),
                  pl.BlockSpec((None, 1, HEAD_DIM), lambda b, s: (layer, 0, 0))],
        out_specs=pl.BlockSpec((ts, W), lambda b, s: (b * ns + s, 0)),
        out_shape=jax.ShapeDtypeStruct((t, W), BF16),
        scratch_shapes=[pltpu.VMEM((ts + 8, 3 * W), F32),
                        pltpu.VMEM((ts, W), F32), pltpu.VMEM((ts, W), F32),
                        pltpu.VMEM((ts, W), F32), pltpu.VMEM((ts, W), F32),
                        pltpu.VMEM((ts, W), F32),
                        pltpu.VMEM((nc, N_HEADS, 2 * GDN_CHUNK, HEAD_DIM), BF16),
                        pltpu.VMEM((nc, N_HEADS // 2, 2 * GDN_CHUNK, 2 * GDN_CHUNK), BF16),
                        pltpu.VMEM((nc, N_HEADS, HEAD_DIM, 2 * GDN_CHUNK), BF16),
                        pltpu.VMEM((nc, 8, LANE), F32),
                        pltpu.VMEM((N_HEADS, HEAD_DIM, HEAD_DIM), F32)],
        compiler_params=_cparams("parallel", "arbitrary"),
        name="gdn_mixer",
    )(proj, proj, proj, conv_w, prm, norm_gain.reshape(norm_gain.shape[0], 1, HEAD_DIM))


def _merge_kernel(yp_ref, ys_ref, yg_ref, gp_ref, gs_ref, gg_ref, wp_ref, ws_ref, wg_ref, o_ref):
    m = jax.nn.sigmoid(gp_ref[...]) * _dot(yp_ref[...], wp_ref[...])
    m = m + jax.nn.sigmoid(gs_ref[...]) * _dot(ys_ref[...], ws_ref[...])
    m = m + jax.nn.sigmoid(gg_ref[...]) * _dot(yg_ref[...], wg_ref[...])
    o_ref[...] = m.astype(o_ref.dtype)


def merge_branches(y_pool, y_sb, y_gdn, proj, w_pool_up, w_sb_up, w_gdn_up, layer, gate_col, tm=1024, tn=1024):
    t = y_pool.shape[0]
    d = w_pool_up.shape[2]
    nb = d // tn
    g0 = _block_index(gate_col, tn)

    def y_spec(y):
        return pl.BlockSpec((tm, y.shape[1]), lambda i, j: (i, 0))

    def w_spec(w):
        return pl.BlockSpec((None, w.shape[1], tn), lambda i, j: (layer, 0, j))

    def gate_spec(branch):
        return pl.BlockSpec((tm, tn), lambda i, j: (i, g0 + branch * nb + j))

    return pl.pallas_call(
        _merge_kernel,
        grid=(t // tm, nb),
        in_specs=[y_spec(y_pool), y_spec(y_sb), y_spec(y_gdn),
                  gate_spec(0), gate_spec(1), gate_spec(2),
                  w_spec(w_pool_up), w_spec(w_sb_up), w_spec(w_gdn_up)],
        out_specs=pl.BlockSpec((tm, tn), lambda i, j: (i, j)),
        out_shape=jax.ShapeDtypeStruct((t, d), BF16),
        compiler_params=_cparams("parallel", "arbitrary"),
        name="merge_branches",
    )(y_pool, y_sb, y_gdn, proj, proj, proj, w_pool_up, w_sb_up, w_gdn_up)


def _matmul_res_kernel(a_ref, w_ref, r_ref, o_ref):
    o_ref[...] = r_ref[...] + _dot(a_ref[...], w_ref[...])


def matmul_residual(a, w, res, layer, tm=1024, tn=1024):
    t, k = a.shape
    n = w.shape[2]
    return pl.pallas_call(
        _matmul_res_kernel,
        grid=(t // tm, n // tn),
        in_specs=[pl.BlockSpec((tm, k), lambda i, j: (i, 0)),
                  pl.BlockSpec((None, k, tn), lambda i, j: (layer, 0, j)),
                  pl.BlockSpec((tm, tn), lambda i, j: (i, j))],
        out_specs=pl.BlockSpec((tm, tn), lambda i, j: (i, j)),
        out_shape=jax.ShapeDtypeStruct((t, n), F32),
        compiler_params=_cparams("parallel", "arbitrary"),
        name="out_proj_residual",
    )(a, w, res)


def _mlp_kernel(x_ref, g_ref, w1_ref, w2_ref, fg_ref, o_ref, u_ref, acc_ref, *, final_norm):
    j = pl.program_id(1)

    @pl.when(j == 0)
    def _():
        u_ref[...] = _rms_norm(x_ref[...], g_ref[...]).astype(BF16)
        acc_ref[...] = jnp.zeros_like(acc_ref)

    h = jnp.maximum(_dot(u_ref[...], w1_ref[...]), 0.0)
    acc_ref[...] += _dot((h * h).astype(BF16), w2_ref[...])

    @pl.when(j == pl.num_programs(1) - 1)
    def _():
        out = x_ref[...] + acc_ref[...]
        if final_norm:
            out = _rms_norm(out, fg_ref[...])
        o_ref[...] = out


def mlp_block(x, gain, w1, w2, final_gain, layer, final_norm, tm=512, tf=1024):
    t, d = x.shape
    f = w1.shape[2]
    kernel = functools.partial(_mlp_kernel, final_norm=final_norm)
    return pl.pallas_call(
        kernel,
        grid=(t // tm, f // tf),
        in_specs=[pl.BlockSpec((tm, d), lambda i, j: (i, 0)),
                  pl.BlockSpec((None, 1, d), lambda i, j: (layer, 0, 0)),
                  pl.BlockSpec((None, d, tf), lambda i, j: (layer, 0, j)),
                  pl.BlockSpec((None, tf, d), lambda i, j: (layer, j, 0)),
                  pl.BlockSpec((1, d), lambda i, j: (0, 0))],
        out_specs=pl.BlockSpec((tm, d), lambda i, j: (i, 0)),
        out_shape=jax.ShapeDtypeStruct((t, d), F32),
        scratch_shapes=[pltpu.VMEM((tm, d), BF16), pltpu.VMEM((tm, d), F32)],
        compiler_params=_cparams("parallel", "arbitrary"),
        name="mlp_block",
    )(x, gain.reshape(gain.shape[0], 1, d), w1, w2, final_gain.reshape(1, d))


def kernel(x, attn_norm, w_in, pool_w, pool_scale, gdn_conv, gdn_a_log, gdn_dt_bias, gdn_norm,
           w_pool_up, w_sb_up, w_gdn_up, w_out, mlp_norm, w_ff1, w_ff2, final_norm):
    batch, seq, d = x.shape
    depth = w_in.shape[0]
    pool_width = pool_w.shape[1] * pool_w.shape[2]
    lay = InProjLayout(pool_width, d)

    w_in_p = repack_in_proj(w_in, lay)
    w_pool_up, w_sb_up, w_gdn_up, w_out, w_ff1, w_ff2 = (
        cast_bf16(w) for w in (w_pool_up, w_sb_up, w_gdn_up, w_out, w_ff1, w_ff2))
    gdn_prm = jnp.pad(jnp.stack([gdn_a_log, gdn_dt_bias], axis=1), ((0, 0), (0, 6), (0, LANE - N_HEADS)))

    xt = x.reshape(batch * seq, d)
    for l in range(depth):
        proj = norm_matmul(xt, attn_norm, w_in_p, l)
        y_pool = pool_mixer(proj, pool_w, pool_scale, l, batch, seq, pool_width, lay.dst_pool)
        y_sb = sb_attention(proj, batch, seq, lay.dst_sb, lay.dst_sb + N_HEADS, lay.dst_sb + 2 * N_HEADS)
        y_gdn = gdn_mixer(proj, gdn_conv, gdn_prm, gdn_norm, l, batch, seq, lay.dst_gdn, lay.dst_z, lay.dst_ab)
        merged = merge_branches(y_pool, y_sb, y_gdn, proj, w_pool_up, w_sb_up, w_gdn_up, l, lay.dst_gate)
        xt = matmul_residual(merged, w_out, xt, l)
        xt = mlp_block(xt, mlp_norm, w_ff1, w_ff2, final_norm, l, final_norm=(l == depth - 1))
    return xt.reshape(batch, seq, d)
```

```python
import functools

import jax
import jax.numpy as jnp
from jax import lax
from jax.experimental import pallas as pl
from jax.experimental.pallas import tpu as pltpu

EPS = 1e-6
LANE = 128
VMEM_LIMIT = 56 * 1024 * 1024

POOL_WINDOWS = (2, 4, 8, 16)
HEAD_DIM = 128
N_HEADS = 6
GDN_CONV = 4
GDN_CHUNK = 64

BF16 = jnp.bfloat16
F32 = jnp.float32


def _cparams(*sem):
    return pltpu.CompilerParams(dimension_semantics=sem, vmem_limit_bytes=VMEM_LIMIT)


def _rms_norm(x, gain):
    return x * lax.rsqrt(jnp.mean(x * x, axis=-1, keepdims=True) + EPS) * gain


def _softplus(x):
    return jnp.maximum(x, 0.0) + jnp.log1p(jnp.exp(-jnp.abs(x)))


def _split3(x):
    h1 = x.astype(BF16)
    r1 = x - h1.astype(F32)
    h2 = r1.astype(BF16)
    h3 = (r1 - h2.astype(F32)).astype(BF16)
    return h1, h2, h3


def _block_index(lane_col, width):
    assert (lane_col * LANE) % width == 0, (lane_col, width)
    return lane_col * LANE // width


def _dot(a, b):
    return jnp.dot(a, b, preferred_element_type=F32)


def _dot_nt(a, b):
    return lax.dot_general(a, b, (((1,), (1,)), ((), ())), preferred_element_type=F32)


CAST_BLOCK_BYTES = 8 * 1024 * 1024


def _cast_kernel(w_ref, o_ref):
    o_ref[...] = w_ref[...].astype(o_ref.dtype)


def cast_bf16(w):
    depth, rows, cols = w.shape
    tr = min(rows, CAST_BLOCK_BYTES // (cols * 4))
    assert rows % tr == 0
    return pl.pallas_call(
        _cast_kernel,
        grid=(depth, rows // tr),
        in_specs=[pl.BlockSpec((None, tr, cols), lambda l, i: (l, i, 0))],
        out_specs=pl.BlockSpec((None, tr, cols), lambda l, i: (l, i, 0)),
        out_shape=jax.ShapeDtypeStruct(w.shape, BF16),
        compiler_params=_cparams("parallel", "parallel"),
        name="cast_bf16",
    )(w)


class InProjLayout:
    def __init__(self, pool_width, d_model):
        w = N_HEADS * HEAD_DIM
        self.n_ab = 2 * N_HEADS
        self.src_pool, self.src_sb, self.src_gdn = 0, pool_width, pool_width + 3 * w
        self.src_z = self.src_gdn + 3 * w
        self.src_ab = self.src_z + w
        self.src_gate = self.src_ab + self.n_ab
        self.n_gate = 3 * d_model
        self.n_src = self.src_gate + self.n_gate
        self.ab_pad = 2 * LANE
        self.dst_gdn = 0
        self.dst_z = self.dst_gdn + 3 * N_HEADS
        self.dst_sb = self.dst_z + N_HEADS
        self.dst_ab = self.dst_sb + 3 * N_HEADS
        self.dst_pool = self.dst_ab + self.ab_pad // LANE
        self.dst_gate = self.dst_pool + pool_width // LANE
        self.n_dst = self.dst_gate * LANE + self.n_gate
        self.shift = self.src_gate % LANE
        self.gate_base = self.src_gate - self.shift
        assert self.src_ab % LANE == 0 and self.gate_base == self.src_ab and self.shift == self.n_ab
        self.tail_start = (self.n_src // LANE) * LANE


def _repack_kernel(w_ref, tail_ref, o_ref, *, lay):
    def copy(dst_col, src, n):
        o_ref[:, dst_col * LANE:dst_col * LANE + n] = w_ref[:, src:src + n]

    w = N_HEADS * HEAD_DIM
    copy(lay.dst_gdn, lay.src_gdn, 3 * w)
    copy(lay.dst_z, lay.src_z, w)
    copy(lay.dst_pool, lay.src_pool, lay.src_sb - lay.src_pool)
    copy(lay.dst_sb, lay.src_sb, 3 * w)
    rows = w_ref.shape[0]
    lane = lax.broadcasted_iota(jnp.int32, (rows, LANE), 1)
    ab0 = lay.dst_ab * LANE
    o_ref[:, ab0:ab0 + LANE] = jnp.where(lane < lay.n_ab, w_ref[:, lay.src_ab:lay.src_ab + LANE].astype(F32), 0.0).astype(BF16)
    o_ref[:, ab0 + LANE:ab0 + lay.ab_pad] = jnp.zeros((rows, lay.ab_pad - LANE), BF16)

    def src_block(col):
        if col + LANE <= lay.tail_start:
            return w_ref[:, col:col + LANE].astype(F32)
        assert col == lay.tail_start
        return tail_ref[...].astype(F32)

    for c in range(lay.n_gate // LANE):
        col = lay.gate_base + c * LANE
        a = pltpu.roll(src_block(col), LANE - lay.shift, 1)
        b = pltpu.roll(src_block(col + LANE), LANE - lay.shift, 1)
        g0 = lay.dst_gate * LANE + c * LANE
        o_ref[:, g0:g0 + LANE] = jnp.where(lane < LANE - lay.shift, a, b).astype(BF16)


def repack_in_proj(w_in, lay, tk=256):
    depth, d, n_src = w_in.shape
    assert n_src == lay.n_src
    w_in = w_in.astype(BF16)
    tail = jnp.pad(w_in[:, :, lay.tail_start:], ((0, 0), (0, 0), (0, LANE - (n_src - lay.tail_start))))
    return pl.pallas_call(
        functools.partial(_repack_kernel, lay=lay),
        grid=(depth, d // tk),
        in_specs=[pl.BlockSpec((None, tk, n_src), lambda l, i: (l, i, 0)),
                  pl.BlockSpec((None, tk, LANE), lambda l, i: (l, i, 0))],
        out_specs=pl.BlockSpec((None, tk, lay.n_dst), lambda l, i: (l, i, 0)),
        out_shape=jax.ShapeDtypeStruct((depth, d, lay.n_dst), BF16),
        compiler_params=_cparams("parallel", "parallel"),
        name="repack_in_proj",
    )(w_in, tail)


def _norm_matmul_kernel(x_ref, g_ref, w_ref, o_ref, u_ref):
    @pl.when(pl.program_id(1) == 0)
    def _():
        u_ref[...] = _rms_norm(x_ref[...], g_ref[...]).astype(BF16)

    o_ref[...] = _dot(u_ref[...], w_ref[...])


def norm_matmul(x, gain, w, layer, tm=1024, tn=1536):
    t, d = x.shape
    n = w.shape[2]
    return pl.pallas_call(
        _norm_matmul_kernel,
        grid=(t // tm, n // tn),
        in_specs=[pl.BlockSpec((tm, d), lambda i, j: (i, 0)),
                  pl.BlockSpec((None, 1, d), lambda i, j: (layer, 0, 0)),
                  pl.BlockSpec((None, d, tn), lambda i, j: (layer, 0, j))],
        out_specs=pl.BlockSpec((tm, tn), lambda i, j: (i, j)),
        out_shape=jax.ShapeDtypeStruct((t, n), F32),
        scratch_shapes=[pltpu.VMEM((tm, d), BF16)],
        compiler_params=_cparams("parallel", "arbitrary"),
        name="norm_in_proj",
    )(x, gain.reshape(gain.shape[0], 1, d), w)


def _pool_kernel(p_ref, w_ref, s_ref, o_ref):
    s, width = p_ref.shape
    group = width // len(POOL_WINDOWS)
    row = lax.broadcasted_iota(jnp.int32, (s, group), 0)
    n_seen = (row + 1).astype(F32)

    def shifted(x, k):
        return jnp.where(row >= k, pltpu.roll(x, k, 0), 0.0)

    for g, win in enumerate(POOL_WINDOWS):
        p = p_ref[:, g * group:(g + 1) * group]
        acc = p
        k = 1
        while k < win:
            acc = acc + shifted(acc, k)
            k *= 2
        d = acc / jnp.minimum(n_seen, float(win)) - p
        y = _dot(d.astype(BF16), w_ref[g].astype(BF16))
        o_ref[:, g * group:(g + 1) * group] = (y * s_ref[:, g * group:(g + 1) * group]).astype(o_ref.dtype)


def pool_mixer(proj, pool_w, pool_scale, layer, batch, seq, width, col):
    t = batch * seq
    _, n_g, group, _ = pool_w.shape
    col_blk = _block_index(col, width)
    return pl.pallas_call(
        _pool_kernel,
        grid=(batch,),
        in_specs=[pl.BlockSpec((seq, width), lambda b: (b, col_blk)),
                  pl.BlockSpec((None, n_g, group, group), lambda b: (layer, 0, 0, 0)),
                  pl.BlockSpec((None, 1, width), lambda b: (layer, 0, 0))],
        out_specs=pl.BlockSpec((seq, width), lambda b: (b, 0)),
        out_shape=jax.ShapeDtypeStruct((t, width), BF16),
        compiler_params=_cparams("parallel"),
        name="pool_mixer",
    )(proj, pool_w, pool_scale.reshape(pool_scale.shape[0], 1, width))


def _sb_kernel(q_ref, k_ref, v_ref, o_ref, *, blk, scale, heads):
    qi = pl.program_id(2)
    r = lax.broadcasted_iota(jnp.int32, (blk, blk), 0)
    c = lax.broadcasted_iota(jnp.int32, (blk, blk), 1)
    later_mat = (r > c).astype(BF16)
    causal = c < r
    qs = [(q_ref[:, h * HEAD_DIM:(h + 1) * HEAD_DIM] * -scale).astype(BF16) for h in range(heads)]

    def block(j, carry, masked):
        start = pl.multiple_of(j * blk, blk)
        hd = [slice(h * HEAD_DIM, (h + 1) * HEAD_DIM) for h in range(heads)]
        nzs = [_dot_nt(qs[h], k_ref[pl.ds(start, blk), hd[h]].astype(BF16)) for h in range(heads)]
        stay, his, los = [], [], []
        for nz in nzs:
            log_stay = jnp.minimum(nz, 0.0) - jnp.log(1.0 + jnp.exp(-jnp.abs(nz)))
            if masked:
                log_stay = jnp.where(causal, log_stay, 0.0)
            hi = log_stay.astype(BF16)
            stay.append(log_stay)
            his.append(hi)
            los.append((log_stay - hi.astype(F32)).astype(BF16))
        laters = [_dot(his[h], later_mat) + _dot(los[h], later_mat) for h in range(heads)]
        accs, runs = [], []
        for h in range(heads):
            a = jnp.exp(stay[h] - nzs[h] + (laters[h] + carry[h][1]))
            if masked:
                a = jnp.where(causal, a, 0.0)
            accs.append(carry[h][0] + _dot(a.astype(BF16), v_ref[pl.ds(start, blk), hd[h]].astype(BF16)))
            runs.append(carry[h][1] + (laters[h][:, 0:1] + stay[h][:, 0:1]))
        return tuple(zip(accs, runs))

    zero = (jnp.zeros((blk, HEAD_DIM), F32), jnp.zeros((blk, 1), F32))
    carry = block(qi, (zero,) * heads, True)
    carry = lax.fori_loop(0, qi, lambda i, cr: block(qi - 1 - i, cr, False), carry)
    for h in range(heads):
        o_ref[:, h * HEAD_DIM:(h + 1) * HEAD_DIM] = carry[h][0].astype(o_ref.dtype)


def sb_attention(proj, batch, seq, q_col, k_col, v_col, blk=256, heads=N_HEADS):
    t = batch * seq
    nq = seq // blk
    width = heads * HEAD_DIM
    qb, kb, vb = (_block_index(col, width) for col in (q_col, k_col, v_col))
    kernel = functools.partial(_sb_kernel, blk=blk, scale=HEAD_DIM ** -0.5, heads=heads)
    return pl.pallas_call(
        kernel,
        grid=(batch, N_HEADS // heads, nq),
        in_specs=[pl.BlockSpec((blk, width), lambda b, h, i: (b * nq + i, qb + h)),
                  pl.BlockSpec((seq, width), lambda b, h, i: (b, kb + h)),
                  pl.BlockSpec((seq, width), lambda b, h, i: (b, vb + h))],
        out_specs=pl.BlockSpec((blk, width), lambda b, h, i: (b * nq + i, h)),
        out_shape=jax.ShapeDtypeStruct((t, N_HEADS * HEAD_DIM), BF16),
        compiler_params=_cparams("parallel", "parallel", "arbitrary"),
        name="sb_attention",
    )(proj, proj, proj)


def _gdn_kernel(qkv_ref, z_ref, ab_ref, conv_ref, prm_ref, gain_ref, o_ref,
                xs_ref, q_s, k_s, v_s, u_s, o_s, wq_s, qk_s, kdt_s, dec_s, state_ref, *, ts):
    C = GDN_CHUNK
    W = N_HEADS * HEAD_DIM
    n_chunks = ts // C
    n_pairs = N_HEADS // 2
    pad = 8

    @pl.when(pl.program_id(1) == 0)
    def _():
        xs_ref[0:pad, :] = jnp.zeros((pad, 3 * W), F32)
        state_ref[...] = jnp.zeros_like(state_ref)

    cur = qkv_ref[...]
    tail = xs_ref[0:pad, :]
    row8 = lax.broadcasted_iota(jnp.int32, (pad, 3 * W), 0)
    y = cur * conv_ref[GDN_CONV - 1:GDN_CONV, :]
    for k in range(1, GDN_CONV):
        r = pltpu.roll(cur, k, 0)
        first = jnp.where(row8 < k, pltpu.roll(tail, k, 0), r[0:pad])
        y = y + jnp.concatenate([first, r[pad:]], axis=0) * conv_ref[GDN_CONV - 1 - k:GDN_CONV - k, :]
    xs_ref[0:pad, :] = cur[ts - pad:ts]
    y = y * jax.nn.sigmoid(y)
    for h in range(N_HEADS):
        hd = slice(h * HEAD_DIM, (h + 1) * HEAD_DIM)
        qh = y[:, hd]
        kh = y[:, W + h * HEAD_DIM:W + (h + 1) * HEAD_DIM]
        q_s[:, hd] = qh * lax.rsqrt(jnp.sum(qh * qh, axis=-1, keepdims=True) + EPS) * (HEAD_DIM ** -0.5)
        k_s[:, hd] = kh * lax.rsqrt(jnp.sum(kh * kh, axis=-1, keepdims=True) + EPS)
    v_s[...] = y[:, 2 * W:]

    ab = ab_ref[...]
    log_alpha = -jnp.exp(prm_ref[0:1, :]) * _softplus(ab + prm_ref[1:2, :])
    rt = lax.broadcasted_iota(jnp.int32, (ts, ts), 0)
    ct = lax.broadcasted_iota(jnp.int32, (ts, ts), 1)
    chunk_tril = ((rt // C == ct // C) & (ct <= rt)).astype(BF16)
    a1, a2, a3 = _split3(log_alpha)
    g_all = _dot(chunk_tril, a1) + _dot(chunk_tril, a2) + _dot(chunk_tril, a3)
    beta_all = jax.nn.sigmoid(ab)

    r2 = lax.broadcasted_iota(jnp.int32, (2 * C, 2 * C), 0)
    c2 = lax.broadcasted_iota(jnp.int32, (2 * C, 2 * C), 1)
    same = (r2 < C) == (c2 < C)
    incl = same & (c2 <= r2)
    strict = same & (c2 < r2)
    eye = (r2 == c2).astype(F32)
    row_first = r2 < C
    lane_first = c2 < C
    lane_first_row = lax.broadcasted_iota(jnp.int32, (1, 2 * C), 1) < C

    hd = [slice(h * HEAD_DIM, (h + 1) * HEAD_DIM) for h in range(N_HEADS)]
    inst = [(ci, p) for ci in range(n_chunks) for p in range(n_pairs)]
    g_ts = []
    for ci in range(n_chunks):
        g_c = g_all[ci * C:(ci + 1) * C]
        g_t = jnp.concatenate([g_c, g_c], axis=0).T
        dec_s[ci] = jnp.broadcast_to(jnp.exp(g_t[0:8, C - 1:C]), (8, LANE))
        g_ts.append(g_t)

    def pair_rows(ref, ci, p):
        rows = slice(ci * C, (ci + 1) * C)
        return jnp.concatenate([ref[rows, hd[2 * p]], ref[rows, hd[2 * p + 1]]], axis=0)

    def col_bcast(x, ci, p, lane0):
        x = x[ci * C:(ci + 1) * C]
        return jnp.concatenate(
            [jnp.broadcast_to(x[:, lane0 + h:lane0 + h + 1], (C, 2 * C)) for h in (2 * p, 2 * p + 1)], axis=0)

    gb = [col_bcast(g_all, ci, p, 0) for ci, p in inst]
    bb = [col_bcast(beta_all, ci, p, N_HEADS) for ci, p in inst]
    gr = [jnp.where(lane_first_row, g_ts[ci][2 * p:2 * p + 1, :], g_ts[ci][2 * p + 1:2 * p + 2, :])
          for ci, p in inst]
    gamma = [jnp.where(incl, jnp.exp(jnp.where(incl, gb[i] - gr[i], 0.0)), 0.0) for i in range(len(inst))]
    k2t = [pair_rows(k_s, ci, p).T for ci, p in inst]
    kq = [_dot(jnp.concatenate([pair_rows(k_s, ci, p) * bb[i], pair_rows(q_s, ci, p)], axis=0).astype(BF16),
               k2t[i].astype(BF16)) for i, (ci, p) in enumerate(inst)]
    pw = [jnp.where(strict, kq[i][:2 * C] * gamma[i], 0.0) for i in range(len(inst))]
    inv = [eye - x for x in pw]
    for _ in range(5):
        pw = [_dot(x.astype(BF16), x.astype(BF16)) for x in pw]
        inv = [y + _dot(y.astype(BF16), x.astype(BF16)) for y, x in zip(inv, pw)]
    eg = [jnp.exp(x) for x in gb]
    rhs = [jnp.concatenate([pair_rows(v_s, ci, p) * bb[i], pair_rows(k_s, ci, p) * (bb[i] * eg[i])], axis=1)
           for i, (ci, p) in enumerate(inst)]
    sol = [rhs[i] + _dot((inv[i] - eye).astype(BF16), rhs[i].astype(BF16)) for i in range(len(inst))]
    for i, (ci, p) in enumerate(inst):
        rows = slice(ci * C, (ci + 1) * C)
        w2 = sol[i][:, HEAD_DIM:].astype(BF16)
        qd2 = (pair_rows(q_s, ci, p) * eg[i]).astype(BF16)
        qk_s[ci, p] = (kq[i][2 * C:] * gamma[i]).astype(BF16)
        g_last = jnp.where(lane_first_row, g_ts[ci][2 * p:2 * p + 1, C - 1:C], g_ts[ci][2 * p + 1:2 * p + 2, C - 1:C])
        kdt = k2t[i] * jnp.exp(g_last - gr[i])
        for n, h in enumerate((2 * p, 2 * p + 1)):
            half = slice(n * C, (n + 1) * C)
            u_s[rows, hd[h]] = sol[i][half, :HEAD_DIM]
            wq_s[ci, h] = jnp.concatenate([w2[half], qd2[half]], axis=0)
            kdt_s[ci, h] = jnp.where(lane_first if n == 0 else ~lane_first, kdt, 0.0).astype(BF16)

    def chunk(ci, _):
        rows = pl.ds(pl.multiple_of(ci * C, C), C)
        dec = dec_s[ci]
        heads = range(N_HEADS)
        rs = [_dot(wq_s[ci, h], state_ref[h].astype(BF16)) for h in heads]
        vn2 = [jnp.concatenate([u_s[rows, hd[h]] - rs[h][:C] for h in (2 * p, 2 * p + 1)], axis=0).astype(BF16)
               for p in range(n_pairs)]
        o2 = [jnp.concatenate([rs[2 * p][C:], rs[2 * p + 1][C:]], axis=0) + _dot(qk_s[ci, p], vn2[p])
              for p in range(n_pairs)]
        new_state = [state_ref[h] * dec[h:h + 1, :] + _dot(kdt_s[ci, h], vn2[h // 2]) for h in heads]
        for h in heads:
            state_ref[h] = new_state[h]
            o_s[rows, hd[h]] = o2[h // 2][(h % 2) * C:(h % 2 + 1) * C]
        return 0

    lax.fori_loop(0, n_chunks, chunk, 0)

    gain = gain_ref[...]
    for h in range(N_HEADS):
        hd = slice(h * HEAD_DIM, (h + 1) * HEAD_DIM)
        zh = z_ref[:, hd]
        o_ref[:, hd] = (_rms_norm(o_s[:, hd], gain) * (zh * jax.nn.sigmoid(zh))).astype(o_ref.dtype)


def gdn_mixer(proj, conv_w, prm, norm_gain, layer, batch, seq, qkv_col, z_col, ab_col, ts=512):
    t = batch * seq
    W = N_HEADS * HEAD_DIM
    ns = seq // ts
    nc = ts // GDN_CHUNK
    qkv_blk = _block_index(qkv_col, 3 * W)
    z_blk = _block_index(z_col, W)
    kernel = functools.partial(_gdn_kernel, ts=ts)
    return pl.pallas_call(
        kernel,
        grid=(batch, ns),
        in_specs=[pl.BlockSpec((ts, 3 * W), lambda b, s: (b * ns + s, qkv_blk)),
                  pl.BlockSpec((ts, W), lambda b, s: (b * ns + s, z_blk)),
                  pl.BlockSpec((ts, LANE), lambda b, s: (b * ns + s, ab_col)),
                  pl.BlockSpec((None, GDN_CONV, 3 * W), lambda b, s: (layer, 0, 0)),
                  pl.BlockSpec((None, 8, LANE), lambda b, s: (layer, 0, 0)),
                  pl.BlockSpec((None, 1, HEAD_DIM), lambda b, s: (layer, 0, 0))],
        out_specs=pl.BlockSpec((ts, W), lambda b, s: (b * ns + s, 0)),
        out_shape=jax.ShapeDtypeStruct((t, W), BF16),
        scratch_shapes=[pltpu.VMEM((ts + 8, 3 * W), F32),
                        pltpu.VMEM((ts, W), F32), pltpu.VMEM((ts, W), F32),
                        pltpu.VMEM((ts, W), F32), pltpu.VMEM((ts, W), F32),
                        pltpu.VMEM((ts, W), F32),
                        pltpu.VMEM((nc, N_HEADS, 2 * GDN_CHUNK, HEAD_DIM), BF16),
                        pltpu.VMEM((nc, N_HEADS // 2, 2 * GDN_CHUNK, 2 * GDN_CHUNK), BF16),
                        pltpu.VMEM((nc, N_HEADS, HEAD_DIM, 2 * GDN_CHUNK), BF16),
                        pltpu.VMEM((nc, 8, LANE), F32),
                        pltpu.VMEM((N_HEADS, HEAD_DIM, HEAD_DIM), F32)],
        compiler_params=_cparams("parallel", "arbitrary"),
        name="gdn_mixer",
    )(proj, proj, proj, conv_w, prm, norm_gain.reshape(norm_gain.shape[0], 1, HEAD_DIM))


def _merge_kernel(yp_ref, ys_ref, yg_ref, gp_ref, gs_ref, gg_ref, wp_ref, ws_ref, wg_ref, o_ref):
    m = jax.nn.sigmoid(gp_ref[...]) * _dot(yp_ref[...], wp_ref[...])
    m = m + jax.nn.sigmoid(gs_ref[...]) * _dot(ys_ref[...], ws_ref[...])
    m = m + jax.nn.sigmoid(gg_ref[...]) * _dot(yg_ref[...], wg_ref[...])
    o_ref[...] = m.astype(o_ref.dtype)


def merge_branches(y_pool, y_sb, y_gdn, proj, w_pool_up, w_sb_up, w_gdn_up, layer, gate_col, tm=1024, tn=1024):
    t = y_pool.shape[0]
    d = w_pool_up.shape[2]
    nb = d // tn
    g0 = _block_index(gate_col, tn)

    def y_spec(y):
        return pl.BlockSpec((tm, y.shape[1]), lambda i, j: (i, 0))

    def w_spec(w):
        return pl.BlockSpec((None, w.shape[1], tn), lambda i, j: (layer, 0, j))

    def gate_spec(branch):
        return pl.BlockSpec((tm, tn), lambda i, j: (i, g0 + branch * nb + j))

    return pl.pallas_call(
        _merge_kernel,
        grid=(t // tm, nb),
        in_specs=[y_spec(y_pool), y_spec(y_sb), y_spec(y_gdn),
                  gate_spec(0), gate_spec(1), gate_spec(2),
                  w_spec(w_pool_up), w_spec(w_sb_up), w_spec(w_gdn_up)],
        out_specs=pl.BlockSpec((tm, tn), lambda i, j: (i, j)),
        out_shape=jax.ShapeDtypeStruct((t, d), BF16),
        compiler_params=_cparams("parallel", "arbitrary"),
        name="merge_branches",
    )(y_pool, y_sb, y_gdn, proj, proj, proj, w_pool_up, w_sb_up, w_gdn_up)


def _matmul_res_kernel(a_ref, w_ref, r_ref, o_ref):
    o_ref[...] = r_ref[...] + _dot(a_ref[...], w_ref[...])


def matmul_residual(a, w, res, layer, tm=1024, tn=1024):
    t, k = a.shape
    n = w.shape[2]
    return pl.pallas_call(
        _matmul_res_kernel,
        grid=(t // tm, n // tn),
        in_specs=[pl.BlockSpec((tm, k), lambda i, j: (i, 0)),
                  pl.BlockSpec((None, k, tn), lambda i, j: (layer, 0, j)),
                  pl.BlockSpec((tm, tn), lambda i, j: (i, j))],
        out_specs=pl.BlockSpec((tm, tn), lambda i, j: (i, j)),
        out_shape=jax.ShapeDtypeStruct((t, n), F32),
        compiler_params=_cparams("parallel", "arbitrary"),
        name="out_proj_residual",
    )(a, w, res)


def _mlp_kernel(x_ref, g_ref, w1_ref, w2_ref, fg_ref, o_ref, u_ref, acc_ref, *, final_norm):
    j = pl.program_id(1)

    @pl.when(j == 0)
    def _():
        u_ref[...] = _rms_norm(x_ref[...], g_ref[...]).astype(BF16)
        acc_ref[...] = jnp.zeros_like(acc_ref)

    h = jnp.maximum(_dot(u_ref[...], w1_ref[...]), 0.0)
    acc_ref[...] += _dot((h * h).astype(BF16), w2_ref[...])

    @pl.when(j == pl.num_programs(1) - 1)
    def _():
        out = x_ref[...] + acc_ref[...]
        if final_norm:
            out = _rms_norm(out, fg_ref[...])
        o_ref[...] = out


def mlp_block(x, gain, w1, w2, final_gain, layer, final_norm, tm=512, tf=1024):
    t, d = x.shape
    f = w1.shape[2]
    kernel = functools.partial(_mlp_kernel, final_norm=final_norm)
    return pl.pallas_call(
        kernel,
        grid=(t // tm, f // tf),
        in_specs=[pl.BlockSpec((tm, d), lambda i, j: (i, 0)),
                  pl.BlockSpec((None, 1, d), lambda i, j: (layer, 0, 0)),
                  pl.BlockSpec((None, d, tf), lambda i, j: (layer, 0, j)),
                  pl.BlockSpec((None, tf, d), lambda i, j: (layer, j, 0)),
                  pl.BlockSpec((1, d), lambda i, j: (0, 0))],
        out_specs=pl.BlockSpec((tm, d), lambda i, j: (i, 0)),
        out_shape=jax.ShapeDtypeStruct((t, d), F32),
        scratch_shapes=[pltpu.VMEM((tm, d), BF16), pltpu.VMEM((tm, d), F32)],
        compiler_params=_cparams("parallel", "arbitrary"),
        name="mlp_block",
    )(x, gain.reshape(gain.shape[0], 1, d), w1, w2, final_gain.reshape(1, d))


def kernel(x, attn_norm, w_in, pool_w, pool_scale, gdn_conv, gdn_a_log, gdn_dt_bias, gdn_norm,
           w_pool_up, w_sb_up, w_gdn_up, w_out, mlp_norm, w_ff1, w_ff2, final_norm):
    batch, seq, d = x.shape
    depth = w_in.shape[0]
    pool_width = pool_w.shape[1] * pool_w.shape[2]
    lay = InProjLayout(pool_width, d)

    w_in_p = repack_in_proj(w_in, lay)
    w_pool_up, w_sb_up, w_gdn_up, w_out, w_ff1, w_ff2 = (
        cast_bf16(w) for w in (w_pool_up, w_sb_up, w_gdn_up, w_out, w_ff1, w_ff2))
    gdn_prm = jnp.pad(jnp.stack([gdn_a_log, gdn_dt_bias], axis=1), ((0, 0), (0, 6), (0, LANE - N_HEADS)))

    xt = x.reshape(batch * seq, d)
    for l in range(depth):
        proj = norm_matmul(xt, attn_norm, w_in_p, l)
        y_pool = pool_mixer(proj, pool_w, pool_scale, l, batch, seq, pool_width, lay.dst_pool)
        y_sb = sb_attention(proj, batch, seq, lay.dst_sb, lay.dst_sb + N_HEADS, lay.dst_sb + 2 * N_HEADS)
        y_gdn = gdn_mixer(proj, gdn_conv, gdn_prm, gdn_norm, l, batch, seq, lay.dst_gdn, lay.dst_z, lay.dst_ab)
        merged = merge_branches(y_pool, y_sb, y_gdn, proj, w_pool_up, w_sb_up, w_gdn_up, l, lay.dst_gate)
        xt = matmul_residual(merged, w_out, xt, l)
        xt = mlp_block(xt, mlp_norm, w_ff1, w_ff2, final_norm, l, final_norm=(l == depth - 1))
    return xt.reshape(batch, seq, d)
```
